```python
import jax, jax.numpy as jnp
from jax import lax
import numpy as np

D_MODEL = 2048
BATCH = 2
SEQ = 4096
DEPTH = 1

ATTN_HEADS = 8
HEAD_DIM = 128
ATTN_WIDTH = ATTN_HEADS * HEAD_DIM
CONV_WIDTH = D_MODEL // 2
CONV_K = 3
MOBA_BLOCK = 256
MOBA_TOP_K = 3
QUERY_CHUNK = 64
D_FF = 4 * D_MODEL
PLE_DIM = 256
RMS_EPS = 1e-6
IN_SIZES = (ATTN_WIDTH, ATTN_WIDTH, ATTN_WIDTH, CONV_WIDTH, CONV_WIDTH, CONV_WIDTH, D_MODEL, D_MODEL)
IN_COLS = sum(IN_SIZES)

kernel_name = 'hybrid_moba_shortconv_block'


def rms_norm(x, g):
    xf = x.astype(jnp.float32)
    y = xf * lax.rsqrt(jnp.mean(xf * xf, axis=-1, keepdims=True) + RMS_EPS)
    return (y * g.astype(jnp.float32)).astype(x.dtype)


def moba_attention(q, k, v):
    B, H, S, hd = q.shape
    dt = q.dtype
    f32 = jnp.float32
    nb = -(-S // MOBA_BLOCK)
    pad = nb * MOBA_BLOCK - S
    kb = jnp.pad(k, ((0, 0), (0, 0), (0, pad), (0, 0))).reshape(B, H, nb, MOBA_BLOCK, hd)
    vb = jnp.pad(v, ((0, 0), (0, 0), (0, pad), (0, 0))).reshape(B, H, nb, MOBA_BLOCK, hd)
    q_blk = jnp.arange(S) // MOBA_BLOCK
    n_sel = min(MOBA_TOP_K, nb - 1)
    scale = HEAD_DIM ** -0.5
    if n_sel > 0:
        k_mean = jnp.mean(kb.astype(f32), axis=3)
        gate = jnp.einsum('bhsd,bhnd->bhsn', q.astype(f32), k_mean)
        past = jnp.arange(nb)[None, :] < q_blk[:, None]
        gate = jnp.where(past[None, None], gate, -jnp.inf)
        _, sel = lax.top_k(gate, n_sel)
        sel_valid = jnp.arange(n_sel)[None, :] < q_blk[:, None]
    bi = jnp.arange(B)[:, None, None, None]
    hi = jnp.arange(H)[None, :, None, None]

    def chunk(c):
        q0 = c * QUERY_CHUNK
        qc = lax.dynamic_slice_in_dim(q, q0, QUERY_CHUNK, axis=2)
        qpos = q0 + jnp.arange(QUERY_CHUNK)
        blk = q0 // MOBA_BLOCK
        k_own = lax.dynamic_index_in_dim(kb, blk, axis=2, keepdims=False)
        v_own = lax.dynamic_index_in_dim(vb, blk, axis=2, keepdims=False)
        kpos = blk * MOBA_BLOCK + jnp.arange(MOBA_BLOCK)
        s_own = jnp.einsum('bhqd,bhkd->bhqk', qc, k_own, preferred_element_type=f32) * scale
        s_own = jnp.where((kpos[None, :] <= qpos[:, None])[None, None], s_own, -jnp.inf)
        if n_sel > 0:
            sel_c = lax.dynamic_slice_in_dim(sel, q0, QUERY_CHUNK, axis=2)
            val_c = lax.dynamic_slice_in_dim(sel_valid, q0, QUERY_CHUNK, axis=0)
            k_sel = kb[bi, hi, sel_c]
            v_sel = vb[bi, hi, sel_c]
            s_sel = jnp.einsum('bhqd,bhqnkd->bhqnk', qc, k_sel, preferred_element_type=f32) * scale
            s_sel = jnp.where(val_c[None, None, :, :, None], s_sel, -jnp.inf)
            s = jnp.concatenate([s_sel.reshape(B, H, QUERY_CHUNK, n_sel * MOBA_BLOCK), s_own], axis=-1)
            probs = jax.nn.softmax(s, axis=-1)
            p_sel = probs[..., :n_sel * MOBA_BLOCK].reshape(B, H, QUERY_CHUNK, n_sel, MOBA_BLOCK)
            p_own = probs[..., n_sel * MOBA_BLOCK:]
            o = (jnp.einsum('bhqnk,bhqnkd->bhqd', p_sel.astype(dt), v_sel, preferred_element_type=f32)
                 + jnp.einsum('bhqk,bhkd->bhqd', p_own.astype(dt), v_own, preferred_element_type=f32))
        else:
            probs = jax.nn.softmax(s_own, axis=-1)
            o = jnp.einsum('bhqk,bhkd->bhqd', probs.astype(dt), v_own, preferred_element_type=f32)
        return o.astype(dt)

    out = lax.map(chunk, jnp.arange(S // QUERY_CHUNK))
    return out.transpose(1, 2, 0, 3, 4).reshape(B, H, S, hd)


def causal_depthwise_conv(u, w):
    C = u.shape[-1]
    return lax.conv_general_dilated(
        u, w[:, None, :].astype(u.dtype), window_strides=(1,), padding=[(CONV_K - 1, 0)],
        dimension_numbers=('NWC', 'WIO', 'NWC'), feature_group_count=C)


def setup_inputs(seed: int = 0) -> dict:
    key = jax.random.key(seed)
    ks = jax.random.split(key, 16)
    f32 = jnp.float32

    def nrm(k, shape, scale):
        return jax.random.normal(k, shape, f32) * scale

    def gain(k, shape):
        return 1.0 + 0.05 * jax.random.normal(k, shape, f32)

    return {
        'x': nrm(ks[0], (BATCH, SEQ, D_MODEL), 1.0),
        'p': nrm(ks[1], (DEPTH, BATCH, SEQ, PLE_DIM), 1.0),
        'g_mix': gain(ks[2], (DEPTH, D_MODEL)),
        'w_in': nrm(ks[3], (DEPTH, D_MODEL, IN_COLS), D_MODEL ** -0.5),
        'g_q': gain(ks[4], (DEPTH, HEAD_DIM)),
        'g_k': gain(ks[5], (DEPTH, HEAD_DIM)),
        'w_conv': nrm(ks[6], (DEPTH, CONV_K, CONV_WIDTH), CONV_K ** -0.5),
        'w_attn_out': nrm(ks[7], (DEPTH, ATTN_WIDTH, D_MODEL), ATTN_WIDTH ** -0.5),
        'w_conv_out': nrm(ks[8], (DEPTH, CONV_WIDTH, D_MODEL), CONV_WIDTH ** -0.5),
        'w_o': nrm(ks[9], (DEPTH, D_MODEL, D_MODEL), D_MODEL ** -0.5),
        'g_mlp': gain(ks[10], (DEPTH, D_MODEL)),
        'w_up': nrm(ks[11], (DEPTH, D_MODEL, D_FF), D_MODEL ** -0.5),
        'w_down': nrm(ks[12], (DEPTH, D_FF, D_MODEL), D_FF ** -0.5),
        'g_ple': gain(ks[13], (DEPTH, D_MODEL)),
        'w_ple_gate': nrm(ks[14], (DEPTH, D_MODEL, D_MODEL), D_MODEL ** -0.5),
        'w_ple_proj': nrm(ks[15], (DEPTH, PLE_DIM, D_MODEL), PLE_DIM ** -0.5),
    }


def reference(x, p, g_mix, w_in, g_q, g_k, w_conv, w_attn_out, w_conv_out, w_o,
              g_mlp, w_up, w_down, g_ple, w_ple_gate, w_ple_proj):
    B, S, D = x.shape
    cuts = [int(c) for c in np.cumsum(IN_SIZES)[:-1]]
    r = x
    for i in range(DEPTH):
        h = rms_norm(r, g_mix[i])
        z = h @ w_in[i]
        q, k, v, c_b, c_c, c_x, g_a, g_c = jnp.split(z, cuts, axis=-1)
        q = rms_norm(q.reshape(B, S, ATTN_HEADS, HEAD_DIM), g_q[i]).transpose(0, 2, 1, 3)
        k = rms_norm(k.reshape(B, S, ATTN_HEADS, HEAD_DIM), g_k[i]).transpose(0, 2, 1, 3)
        v = v.reshape(B, S, ATTN_HEADS, HEAD_DIM).transpose(0, 2, 1, 3)
        attn = moba_attention(q, k, v)
        y_attn = attn.transpose(0, 2, 1, 3).reshape(B, S, ATTN_WIDTH) @ w_attn_out[i]
        y_conv = (c_b * causal_depthwise_conv(c_c * c_x, w_conv[i])) @ w_conv_out[i]
        merged = jax.nn.sigmoid(g_a) * y_attn + jax.nn.sigmoid(g_c) * y_conv
        r = r + merged @ w_o[i]
        h = rms_norm(r, g_mlp[i])
        r = r + jnp.square(jax.nn.relu(h @ w_up[i])) @ w_down[i]
        h = rms_norm(r, g_ple[i])
        r = r + jax.nn.sigmoid(h @ w_ple_gate[i]) * (p[i] @ w_ple_proj[i])
    return r
```

```python
import functools

import jax
import jax.numpy as jnp
from jax import lax
from jax.experimental import pallas as pl
from jax.experimental.pallas import tpu as pltpu

F32 = jnp.float32
BF16 = jnp.bfloat16

D_MODEL = 2048
N_HEADS = 8
HEAD_DIM = 128
ATTN_WIDTH = N_HEADS * HEAD_DIM
CONV_WIDTH = 1024
CONV_K = 3
MOBA_BLOCK = 256
MOBA_TOP_K = 3
D_FF = 4 * D_MODEL
PLE_DIM = 256
RMS_EPS = 1e-6
IN_COLS = 3 * ATTN_WIDTH + 3 * CONV_WIDTH + 2 * D_MODEL
ATTN_SCALE = HEAD_DIM ** -0.5

LANES = 128
SUBLANES = 8
MIB = 1024 * 1024

IN_TN = 512
_Q_END = ATTN_WIDTH // IN_TN
_K_END = 2 * ATTN_WIDTH // IN_TN
_V_END = 3 * ATTN_WIDTH // IN_TN
_CB_END = _V_END + CONV_WIDTH // IN_TN
_CC_END = _CB_END + CONV_WIDTH // IN_TN
_CX_END = _CC_END + CONV_WIDTH // IN_TN
_IN_STEPS = IN_COLS // IN_TN
Z_COLS = IN_COLS - CONV_WIDTH
Z_Q, Z_K, Z_V = 0, ATTN_WIDTH, 2 * ATTN_WIDTH
Z_CB = 3 * ATTN_WIDTH
Z_U = Z_CB + CONV_WIDTH
Z_GA = Z_U + CONV_WIDTH
Z_GC = Z_GA + D_MODEL


def _rms_normalize(x):
    return x * lax.rsqrt(jnp.mean(x * x, axis=-1, keepdims=True) + RMS_EPS)


def _dot(a, b):
    return jnp.dot(a, b, preferred_element_type=F32)


def _dot_nt(a, b):
    return lax.dot_general(a, b, (((1,), (1,)), ((), ())), preferred_element_type=F32)


def _in_proj_kernel(x_ref, gmix_ref, gq_ref, gk_ref, w_ref, z_ref, h_ref, cc_ref):
    j = pl.program_id(1)

    @pl.when(j == 0)
    def _():
        h_ref[...] = (_rms_normalize(x_ref[...]) * gmix_ref[...]).astype(BF16)

    acc = _dot(h_ref[...], w_ref[...].astype(BF16))

    def head_norm(gain):
        for c in range(IN_TN // HEAD_DIM):
            sl = slice(c * HEAD_DIM, (c + 1) * HEAD_DIM)
            z_ref[:, sl] = (_rms_normalize(acc[:, sl]) * gain).astype(BF16)

    @pl.when(j < _Q_END)
    def _():
        head_norm(gq_ref[...] * ATTN_SCALE)

    @pl.when((j >= _Q_END) & (j < _K_END))
    def _():
        head_norm(gk_ref[...])

    @pl.when((j >= _K_END) & (j < _CB_END))
    def _():
        z_ref[...] = acc.astype(BF16)

    @pl.when((j >= _CB_END) & (j < _CC_END))
    def _():
        cc_ref[j - _CB_END] = acc

    @pl.when((j >= _CC_END) & (j < _CX_END))
    def _():
        z_ref[...] = (cc_ref[j - _CC_END] * acc).astype(BF16)

    @pl.when(j >= _CX_END)
    def _():
        z_ref[...] = jax.nn.sigmoid(acc).astype(BF16)


def _in_proj(x2, g_mix, g_q, g_k, w_in, *, tm):
    m = x2.shape[0]
    n_cc = _CC_END - _CB_END

    def z_index(i, j):
        return i, jnp.where(j < _CB_END, j, jnp.maximum(j - n_cc, _CB_END - 1))

    vmem = (2 * tm * D_MODEL * 4 + tm * D_MODEL * 2 + 2 * D_MODEL * IN_TN * 4 + D_MODEL * IN_TN * 2
            + 2 * tm * IN_TN * 2 + 2 * tm * IN_TN * 4 + n_cc * tm * IN_TN * 4)
    return pl.pallas_call(
        _in_proj_kernel,
        out_shape=jax.ShapeDtypeStruct((m, Z_COLS), BF16),
        grid=(m // tm, _IN_STEPS),
        in_specs=[
            pl.BlockSpec((tm, D_MODEL), lambda i, j: (i, 0)),
            pl.BlockSpec((1, D_MODEL), lambda i, j: (0, 0)),
            pl.BlockSpec((1, HEAD_DIM), lambda i, j: (0, 0)),
            pl.BlockSpec((1, HEAD_DIM), lambda i, j: (0, 0)),
            pl.BlockSpec((D_MODEL, IN_TN), lambda i, j: (0, j)),
        ],
        out_specs=pl.BlockSpec((tm, IN_TN), z_index),
        scratch_shapes=[pltpu.VMEM((tm, D_MODEL), BF16), pltpu.VMEM((n_cc, tm, IN_TN), F32)],
        compiler_params=pltpu.CompilerParams(
            dimension_semantics=("arbitrary", "arbitrary"), vmem_limit_bytes=vmem + 8 * MIB),
        name="in_proj",
    )(x2, g_mix, g_q, g_k, w_in)


def _moba_kernel(q_ref, k_ref, v_ref, o_ref, kmean_ref):
    j = pl.program_id(2)
    nb = k_ref.shape[1] // MOBA_BLOCK
    neg_inf = F32(-jnp.inf)

    @pl.when(j == 0)
    def _():
        kmean_ref[...] = jnp.zeros_like(kmean_ref)
        for n in range(nb):
            kb = k_ref[0, n * MOBA_BLOCK:(n + 1) * MOBA_BLOCK, :].astype(F32)
            kmean_ref[n:n + 1, :] = jnp.mean(kb, axis=0, keepdims=True)

    q = q_ref[0]
    kmean = kmean_ref[...]
    km_hi = kmean.astype(BF16)
    km_lo = (kmean - km_hi.astype(F32)).astype(BF16)
    gate = _dot_nt(q, km_hi) + _dot_nt(q, km_lo)
    lane = lax.broadcasted_iota(jnp.int32, gate.shape, 1)
    gate = jnp.where(lane < j, gate, neg_inf)
    sel = jnp.zeros(gate.shape, jnp.bool_)
    for t in range(MOBA_TOP_K):
        best = jnp.max(gate, axis=1, keepdims=True)
        first = jnp.min(jnp.where(gate == best, lane, LANES), axis=1, keepdims=True)
        pick = lane == first
        sel = sel | (pick & (t < j))
        gate = jnp.where(pick, neg_inf, gate)
    bias = jnp.where(sel, F32(0), neg_inf)

    own = pl.multiple_of(j * MOBA_BLOCK, MOBA_BLOCK)
    s = _dot_nt(q, k_ref[0, pl.ds(own, MOBA_BLOCK), :])
    row = lax.broadcasted_iota(jnp.int32, s.shape, 0)
    col = lax.broadcasted_iota(jnp.int32, s.shape, 1)
    s = jnp.where(col <= row, s, neg_inf)
    m0 = jnp.max(s, axis=1, keepdims=True)
    p = jnp.exp(s - m0)
    l0 = jnp.sum(p, axis=1, keepdims=True)
    acc0 = _dot(p.astype(BF16), v_ref[0, pl.ds(own, MOBA_BLOCK), :])

    def body(n, carry):
        m, l, acc = carry
        off = pl.multiple_of(n * MOBA_BLOCK, MOBA_BLOCK)
        bias_n = jnp.max(jnp.where(lane == n, bias, neg_inf), axis=1, keepdims=True)
        s = _dot_nt(q, k_ref[0, pl.ds(off, MOBA_BLOCK), :]) + bias_n
        m_new = jnp.maximum(m, jnp.max(s, axis=1, keepdims=True))
        alpha = jnp.exp(m - m_new)
        p = jnp.exp(s - m_new)
        l = alpha * l + jnp.sum(p, axis=1, keepdims=True)
        acc = alpha * acc + _dot(p.astype(BF16), v_ref[0, pl.ds(off, MOBA_BLOCK), :])
        return m_new, l, acc

    _, l, acc = lax.fori_loop(0, j, body, (m0, l0, acc0))
    o_ref[0] = (acc / l).astype(BF16)


def _moba(z3):
    b, s, _ = z3.shape
    nq = s // MOBA_BLOCK
    kq, kk, kv = Z_Q // HEAD_DIM, Z_K // HEAD_DIM, Z_V // HEAD_DIM
    return pl.pallas_call(
        _moba_kernel,
        out_shape=jax.ShapeDtypeStruct((b, s, ATTN_WIDTH), BF16),
        grid=(b, N_HEADS, nq),
        in_specs=[
            pl.BlockSpec((1, MOBA_BLOCK, HEAD_DIM), lambda bi, h, j: (bi, j, kq + h)),
            pl.BlockSpec((1, s, HEAD_DIM), lambda bi, h, j: (bi, 0, kk + h)),
            pl.BlockSpec((1, s, HEAD_DIM), lambda bi, h, j: (bi, 0, kv + h)),
        ],
        out_specs=pl.BlockSpec((1, MOBA_BLOCK, HEAD_DIM), lambda bi, h, j: (bi, j, h)),
        scratch_shapes=[pltpu.VMEM((LANES, HEAD_DIM), F32)],
        compiler_params=pltpu.CompilerParams(
            dimension_semantics=("arbitrary", "arbitrary", "arbitrary")),
        name="moba",
    )(z3, z3, z3)


def _merge_kernel(seq_tiles, attn_ref, cb_ref, u_ref, halo_ref, wconv_ref, sa_ref, sc_ref,
                  wa_ref, wc_ref, o_ref, ubuf_ref, cin_ref):
    i = pl.program_id(0)
    j = pl.program_id(1)
    tm = attn_ref.shape[0]

    @pl.when(j == 0)
    def _():
        halo = halo_ref[...].astype(F32)
        ubuf_ref[0:SUBLANES, :] = jnp.where(i % seq_tiles == 0, jnp.zeros_like(halo), halo)
        ubuf_ref[SUBLANES:, :] = u_ref[...].astype(F32)
        w = wconv_ref[...]
        conv = (w[0:1, :] * ubuf_ref[SUBLANES - 2:SUBLANES - 2 + tm, :]
                + w[1:2, :] * ubuf_ref[SUBLANES - 1:SUBLANES - 1 + tm, :]
                + w[2:3, :] * ubuf_ref[SUBLANES:, :])
        cin_ref[...] = (cb_ref[...].astype(F32) * conv).astype(BF16)

    y_attn = _dot(attn_ref[...], wa_ref[...].astype(BF16))
    y_conv = _dot(cin_ref[...], wc_ref[...].astype(BF16))
    o_ref[...] = (sa_ref[...].astype(F32) * y_attn + sc_ref[...].astype(F32) * y_conv).astype(BF16)


def _merge(attn2, z2, w_conv, w_attn_out, w_conv_out, *, tm, tn, seq):
    m = attn2.shape[0]
    cw = CONV_WIDTH
    halo_blocks = tm // SUBLANES
    vmem = (2 * tm * ATTN_WIDTH * 2 + 4 * tm * cw * 2 + (tm + SUBLANES) * cw * 4 + tm * cw * 2
            + 4 * tm * tn * 2 + 4 * ATTN_WIDTH * tn * 4 + 2 * ATTN_WIDTH * tn * 2
            + 2 * tm * tn * 2 + 3 * tm * tn * 4 + 3 * tm * cw * 4)
    return pl.pallas_call(
        functools.partial(_merge_kernel, seq // tm),
        out_shape=jax.ShapeDtypeStruct((m, D_MODEL), BF16),
        grid=(m // tm, D_MODEL // tn),
        in_specs=[
            pl.BlockSpec((tm, ATTN_WIDTH), lambda i, j: (i, 0)),
            pl.BlockSpec((tm, cw), lambda i, j: (i, Z_CB // cw)),
            pl.BlockSpec((tm, cw), lambda i, j: (i, Z_U // cw)),
            pl.BlockSpec((SUBLANES, cw), lambda i, j: (jnp.maximum(i * halo_blocks - 1, 0), Z_U // cw)),
            pl.BlockSpec((CONV_K, cw), lambda i, j: (0, 0)),
            pl.BlockSpec((tm, tn), lambda i, j: (i, Z_GA // tn + j)),
            pl.BlockSpec((tm, tn), lambda i, j: (i, Z_GC // tn + j)),
            pl.BlockSpec((ATTN_WIDTH, tn), lambda i, j: (0, j)),
            pl.BlockSpec((cw, tn), lambda i, j: (0, j)),
        ],
        out_specs=pl.BlockSpec((tm, tn), lambda i, j: (i, j)),
        scratch_shapes=[pltpu.VMEM((tm + SUBLANES, cw), F32), pltpu.VMEM((tm, cw), BF16)],
        compiler_params=pltpu.CompilerParams(
            dimension_semantics=("arbitrary", "arbitrary"), vmem_limit_bytes=vmem + 8 * MIB),
        name="merge",
    )(attn2, z2, z2, z2, w_conv, z2, z2, w_attn_out, w_conv_out)


def _out_proj_kernel(a_ref, w_ref, x_ref, o_ref):
    o_ref[...] = x_ref[...] + _dot(a_ref[...], w_ref[...].astype(BF16))


def _out_proj(merged, w_o, x2, *, tm, tn):
    m = merged.shape[0]
    vmem = (2 * tm * D_MODEL * 2 + 2 * D_MODEL * tn * 4 + D_MODEL * tn * 2 + 5 * tm * tn * 4)
    return pl.pallas_call(
        _out_proj_kernel,
        out_shape=jax.ShapeDtypeStruct((m, D_MODEL), F32),
        grid=(m // tm, D_MODEL // tn),
        in_specs=[
            pl.BlockSpec((tm, D_MODEL), lambda i, j: (i, 0)),
            pl.BlockSpec((D_MODEL, tn), lambda i, j: (0, j)),
            pl.BlockSpec((tm, tn), lambda i, j: (i, j)),
        ],
        out_specs=pl.BlockSpec((tm, tn), lambda i, j: (i, j)),
        compiler_params=pltpu.CompilerParams(
            dimension_semantics=("arbitrary", "arbitrary"), vmem_limit_bytes=vmem + 8 * MIB),
        name="out_proj",
    )(merged, w_o, x2)


def _mlp_kernel(r_ref, g_ref, wu_ref, wd_ref, o_ref, h_ref):
    f = pl.program_id(1)

    @pl.when(f == 0)
    def _():
        r = r_ref[...]
        h_ref[...] = (_rms_normalize(r) * g_ref[...]).astype(BF16)
        o_ref[...] = r

    a = jnp.maximum(_dot(h_ref[...], wu_ref[...].astype(BF16)), 0.0)
    o_ref[...] += _dot((a * a).astype(BF16), wd_ref[...].astype(BF16))


def _mlp(r, g_mlp, w_up, w_down, *, tm, tf):
    m = r.shape[0]
    vmem = (4 * tm * D_MODEL * 4 + tm * D_MODEL * 2 + 4 * D_MODEL * tf * 4 + 2 * D_MODEL * tf * 2
            + 3 * tm * tf * 4)
    return pl.pallas_call(
        _mlp_kernel,
        out_shape=jax.ShapeDtypeStruct((m, D_MODEL), F32),
        grid=(m // tm, D_FF // tf),
        in_specs=[
            pl.BlockSpec((tm, D_MODEL), lambda i, f: (i, 0)),
            pl.BlockSpec((1, D_MODEL), lambda i, f: (0, 0)),
            pl.BlockSpec((D_MODEL, tf), lambda i, f: (0, f)),
            pl.BlockSpec((tf, D_MODEL), lambda i, f: (f, 0)),
        ],
        out_specs=pl.BlockSpec((tm, D_MODEL), lambda i, f: (i, 0)),
        scratch_shapes=[pltpu.VMEM((tm, D_MODEL), BF16)],
        compiler_params=pltpu.CompilerParams(
            dimension_semantics=("arbitrary", "arbitrary"), vmem_limit_bytes=vmem + 8 * MIB),
        name="mlp",
    )(r, g_mlp, w_up, w_down)


def _ple_kernel(r_ref, g_ref, p_ref, wg_ref, wp_ref, o_ref, h_ref, pb_ref):
    j = pl.program_id(1)
    tn = o_ref.shape[1]

    @pl.when(j == 0)
    def _():
        h_ref[...] = (_rms_normalize(r_ref[...]) * g_ref[...]).astype(BF16)
        pb_ref[...] = p_ref[...].astype(BF16)

    gate = jax.nn.sigmoid(_dot(h_ref[...], wg_ref[...].astype(BF16)))
    proj = _dot(pb_ref[...], wp_ref[...].astype(BF16))
    col = pl.multiple_of(j * tn, tn)
    o_ref[...] = r_ref[:, pl.ds(col, tn)] + gate * proj


def _ple(r, g_ple, p2, w_gate, w_proj, *, tm, tn):
    m = r.shape[0]
    vmem = (2 * tm * D_MODEL * 4 + tm * D_MODEL * 2 + 2 * tm * PLE_DIM * 4 + tm * PLE_DIM * 2
            + 2 * D_MODEL * tn * 4 + D_MODEL * tn * 2 + 2 * PLE_DIM * tn * 4 + 5 * tm * tn * 4)
    return pl.pallas_call(
        _ple_kernel,
        out_shape=jax.ShapeDtypeStruct((m, D_MODEL), F32),
        grid=(m // tm, D_MODEL // tn),
        in_specs=[
            pl.BlockSpec((tm, D_MODEL), lambda i, j: (i, 0)),
            pl.BlockSpec((1, D_MODEL), lambda i, j: (0, 0)),
            pl.BlockSpec((tm, PLE_DIM), lambda i, j: (i, 0)),
            pl.BlockSpec((D_MODEL, tn), lambda i, j: (0, j)),
            pl.BlockSpec((PLE_DIM, tn), lambda i, j: (0, j)),
        ],
        out_specs=pl.BlockSpec((tm, tn), lambda i, j: (i, j)),
        scratch_shapes=[pltpu.VMEM((tm, D_MODEL), BF16), pltpu.VMEM((tm, PLE_DIM), BF16)],
        compiler_params=pltpu.CompilerParams(
            dimension_semantics=("arbitrary", "arbitrary"), vmem_limit_bytes=vmem + 8 * MIB),
        name="ple",
    )(r, g_ple, p2, w_gate, w_proj)


def kernel(x, p, g_mix, w_in, g_q, g_k, w_conv, w_attn_out, w_conv_out, w_o, g_mlp, w_up, w_down,
           g_ple, w_ple_gate, w_ple_proj):
    b, s, d = x.shape
    depth = p.shape[0]
    m = b * s
    r = x.reshape(m, d)
    for i in range(depth):
        z = _in_proj(r, g_mix[i][None], g_q[i][None], g_k[i][None], w_in[i], tm=1024)
        attn = _moba(z.reshape(b, s, Z_COLS)).reshape(m, ATTN_WIDTH)
        merged = _merge(attn, z, w_conv[i], w_attn_out[i], w_conv_out[i], tm=1024, tn=512, seq=s)
        r = _out_proj(merged, w_o[i], r, tm=1024, tn=512)
        r = _mlp(r, g_mlp[i][None], w_up[i], w_down[i], tm=1024, tf=256)
        r = _ple(r, g_ple[i][None], p[i].reshape(m, PLE_DIM), w_ple_gate[i], w_ple_proj[i],
                 tm=1024, tn=512)
    return r.reshape(b, s, d)
```

```python
import functools
import math

import jax
import jax.numpy as jnp
from jax import lax
from jax.experimental import pallas as pl
from jax.experimental.pallas import tpu as pltpu

F32 = jnp.float32
BF16 = jnp.bfloat16

D_MODEL = 2048
N_HEADS = 8
HEAD_DIM = 128
ATTN_WIDTH = N_HEADS * HEAD_DIM
CONV_WIDTH = 1024
CONV_K = 3
MOBA_BLOCK = 256
MOBA_TOP_K = 3
D_FF = 4 * D_MODEL
PLE_DIM = 256
RMS_EPS = 1e-6
IN_COLS = 3 * ATTN_WIDTH + 3 * CONV_WIDTH + 2 * D_MODEL
Q_PRESCALE = HEAD_DIM ** -0.5 * math.log2(math.e)

LANES = 128
SUBLANES = 8
MIB = 1024 * 1024

IN_TN = 1024
IN_SUB = 512
_STEP_Q, _STEP_K, _STEP_V, _STEP_CB, _STEP_CC, _STEP_CX, _STEP_GATES = 0, 1, 2, 3, 4, 5, 6
_IN_STEPS = IN_COLS // IN_TN
Z_COLS = IN_COLS - CONV_WIDTH
Z_Q, Z_K, Z_V = 0, ATTN_WIDTH, 2 * ATTN_WIDTH
Z_CB = 3 * ATTN_WIDTH
Z_U = Z_CB + CONV_WIDTH
Z_GA = Z_U + CONV_WIDTH
Z_GC = Z_GA + D_MODEL


def _rms_normalize(x):
    return x * lax.rsqrt(jnp.mean(x * x, axis=-1, keepdims=True) + RMS_EPS)


def _dot(a, b):
    return jnp.dot(a, b, preferred_element_type=F32)


def _dot_nt(a, b):
    return lax.dot_general(a, b, (((1,), (1,)), ((), ())), preferred_element_type=F32)


def _in_proj_kernel(x_ref, gmix_ref, gq_ref, gk_ref, w_ref, z_ref, h_ref, cc_ref):
    j = pl.program_id(1)

    @pl.when(j == 0)
    def _():
        h_ref[...] = (_rms_normalize(x_ref[...]) * gmix_ref[...]).astype(BF16)

    def for_each_sub(epilogue):
        for c in range(IN_TN // IN_SUB):
            sl = slice(c * IN_SUB, (c + 1) * IN_SUB)
            epilogue(sl, _dot(h_ref[...], w_ref[:, sl].astype(BF16)))

    def head_norm(gain):
        def epilogue(sl, acc):
            for c in range(IN_SUB // HEAD_DIM):
                hs = slice(c * HEAD_DIM, (c + 1) * HEAD_DIM)
                z_ref[:, sl.start + hs.start:sl.start + hs.stop] = (
                    _rms_normalize(acc[:, hs]) * gain).astype(BF16)
        return epilogue

    @pl.when(j == _STEP_Q)
    def _():
        for_each_sub(head_norm(gq_ref[...] * Q_PRESCALE))

    @pl.when(j == _STEP_K)
    def _():
        for_each_sub(head_norm(gk_ref[...]))

    @pl.when((j == _STEP_V) | (j == _STEP_CB))
    def _():
        def epilogue(sl, acc):
            z_ref[:, sl] = acc.astype(BF16)
        for_each_sub(epilogue)

    @pl.when(j == _STEP_CC)
    def _():
        def epilogue(sl, acc):
            cc_ref[:, sl] = acc
        for_each_sub(epilogue)

    @pl.when(j == _STEP_CX)
    def _():
        def epilogue(sl, acc):
            z_ref[:, sl] = (cc_ref[:, sl] * acc).astype(BF16)
        for_each_sub(epilogue)

    @pl.when(j >= _STEP_GATES)
    def _():
        def epilogue(sl, acc):
            z_ref[:, sl] = jax.nn.sigmoid(acc).astype(BF16)
        for_each_sub(epilogue)


def _in_proj(x2, g_mix, g_q, g_k, w_in, *, tm):
    m = x2.shape[0]

    def z_index(i, j):
        return i, jnp.where(j < _STEP_CC, j, j - 1)

    vmem = (2 * tm * D_MODEL * 4 + tm * D_MODEL * 2 + 2 * D_MODEL * IN_TN * 4 + D_MODEL * IN_SUB * 2
            + 2 * tm * IN_TN * 2 + tm * IN_TN * 4 + 2 * tm * IN_SUB * 4)
    return pl.pallas_call(
        _in_proj_kernel,
        out_shape=jax.ShapeDtypeStruct((m, Z_COLS), BF16),
        grid=(m // tm, _IN_STEPS),
        in_specs=[
            pl.BlockSpec((tm, D_MODEL), lambda i, j: (i, 0)),
            pl.BlockSpec((1, D_MODEL), lambda i, j: (0, 0)),
            pl.BlockSpec((1, HEAD_DIM), lambda i, j: (0, 0)),
            pl.BlockSpec((1, HEAD_DIM), lambda i, j: (0, 0)),
            pl.BlockSpec((D_MODEL, IN_TN), lambda i, j: (0, j)),
        ],
        out_specs=pl.BlockSpec((tm, IN_TN), z_index),
        scratch_shapes=[pltpu.VMEM((tm, D_MODEL), BF16), pltpu.VMEM((tm, IN_TN), F32)],
        compiler_params=pltpu.CompilerParams(
            dimension_semantics=("arbitrary", "arbitrary"), vmem_limit_bytes=vmem + 6 * MIB),
        name="in_proj",
    )(x2, g_mix, g_q, g_k, w_in)


def _moba_kernel(q_ref, k_ref, v_ref, o_ref, kmean_ref, vt_ref, bias_ref, m_ref, l_ref, acc_ref):
    j = pl.program_id(1)
    nb = k_ref.shape[1] // MOBA_BLOCK
    heads = k_ref.shape[2] // HEAD_DIM
    neg_inf = F32(-jnp.inf)

    def head_cols(h):
        return slice(h * HEAD_DIM, (h + 1) * HEAD_DIM)

    @pl.when(j == 0)
    def _():
        for h in range(heads):
            for n in range(nb):
                blk = slice(n * MOBA_BLOCK, (n + 1) * MOBA_BLOCK)
                kmean_ref[h, n:n + 1, :] = jnp.mean(
                    k_ref[0, blk, head_cols(h)].astype(F32), axis=0, keepdims=True)
                vt_ref[head_cols(h), blk] = v_ref[0, blk, head_cols(h)].T

    for h in range(heads):
        q = q_ref[0, :, head_cols(h)]
        kmean = kmean_ref[h]
        km_hi = kmean.astype(BF16)
        km_lo = (kmean - km_hi.astype(F32)).astype(BF16)
        gate = _dot_nt(km_hi, q) + _dot_nt(km_lo, q)
        blk_id = lax.broadcasted_iota(jnp.int32, gate.shape, 0)
        gate = jnp.where(blk_id < j, gate, neg_inf)
        sel = jnp.zeros(gate.shape, jnp.bool_)
        for t in range(MOBA_TOP_K):
            best = jnp.max(gate, axis=0, keepdims=True)
            first = jnp.min(jnp.where(gate == best, blk_id, nb), axis=0, keepdims=True)
            pick = blk_id == first
            sel = sel | (pick & (t < j))
            gate = jnp.where(pick, neg_inf, gate)
        bias_ref[h] = jnp.where(sel, F32(0), neg_inf)

    def attend(off, mask, first):
        scores = [_dot_nt(k_ref[0, pl.ds(off, MOBA_BLOCK), head_cols(h)], q_ref[0, :, head_cols(h)])
                  for h in range(heads)]
        probs, alphas = [], []
        for h in range(heads):
            s = mask(h, scores[h])
            m_blk = jnp.max(s, axis=0, keepdims=True)
            if first:
                m_new, alpha = m_blk, None
            else:
                m_old = m_ref[h]
                m_new = jnp.maximum(m_old, m_blk)
                alpha = jnp.exp2(m_old - m_new)
            p = jnp.exp2(s - m_new)
            l_blk = jnp.sum(p, axis=0, keepdims=True)
            m_ref[h] = m_new
            l_ref[h] = l_blk if first else alpha * l_ref[h] + l_blk
            probs.append(p.astype(BF16))
            alphas.append(alpha)
        for h in range(heads):
            pv = _dot(vt_ref[head_cols(h), pl.ds(off, MOBA_BLOCK)], probs[h])
            acc_ref[h] = pv if first else alphas[h] * acc_ref[h] + pv

    def causal_mask(h, s):
        key_i = lax.broadcasted_iota(jnp.int32, s.shape, 0)
        qry_i = lax.broadcasted_iota(jnp.int32, s.shape, 1)
        return jnp.where(key_i <= qry_i, s, neg_inf)

    attend(pl.multiple_of(j * MOBA_BLOCK, MOBA_BLOCK), causal_mask, True)

    def body(n, carry):
        attend(pl.multiple_of(n * MOBA_BLOCK, MOBA_BLOCK),
               lambda h, s: s + bias_ref[h, pl.ds(n, 1), :], False)
        return carry

    lax.fori_loop(0, j, body, 0)
    for h in range(heads):
        o_ref[0, :, head_cols(h)] = (acc_ref[h] / l_ref[h]).T.astype(BF16)


def _moba(z3):
    b, s, _ = z3.shape
    nq = s // MOBA_BLOCK
    aw = ATTN_WIDTH
    vmem = (4 * s * aw * 2 + s * aw * 2 + 4 * MOBA_BLOCK * aw * 2
            + N_HEADS * (HEAD_DIM + 2 * nq + 2 * SUBLANES) * MOBA_BLOCK * 4)
    return pl.pallas_call(
        _moba_kernel,
        out_shape=jax.ShapeDtypeStruct((b, s, aw), BF16),
        grid=(b, nq),
        in_specs=[
            pl.BlockSpec((1, MOBA_BLOCK, aw), lambda bi, j: (bi, j, Z_Q // aw)),
            pl.BlockSpec((1, s, aw), lambda bi, j: (bi, 0, Z_K // aw)),
            pl.BlockSpec((1, s, aw), lambda bi, j: (bi, 0, Z_V // aw)),
        ],
        out_specs=pl.BlockSpec((1, MOBA_BLOCK, aw), lambda bi, j: (bi, j, 0)),
        scratch_shapes=[
            pltpu.VMEM((N_HEADS, nq, HEAD_DIM), F32),
            pltpu.VMEM((aw, s), BF16),
            pltpu.VMEM((N_HEADS, nq, MOBA_BLOCK), F32),
            pltpu.VMEM((N_HEADS, 1, MOBA_BLOCK), F32),
            pltpu.VMEM((N_HEADS, 1, MOBA_BLOCK), F32),
            pltpu.VMEM((N_HEADS, HEAD_DIM, MOBA_BLOCK), F32),
        ],
        compiler_params=pltpu.CompilerParams(
            dimension_semantics=("arbitrary", "arbitrary"), vmem_limit_bytes=vmem + 6 * MIB),
        name="moba",
    )(z3, z3, z3)


def _merge_kernel(seq_tiles, attn_ref, cb_ref, u_ref, halo_ref, wconv_ref, sa_ref, sc_ref,
                  wa_ref, wc_ref, o_ref, ubuf_ref, cin_ref):
    i = pl.program_id(0)
    j = pl.program_id(1)
    tm = attn_ref.shape[0]

    @pl.when(j == 0)
    def _():
        halo = halo_ref[...].astype(F32)
        ubuf_ref[0:SUBLANES, :] = jnp.where(i % seq_tiles == 0, jnp.zeros_like(halo), halo)
        ubuf_ref[SUBLANES:, :] = u_ref[...].astype(F32)
        w = wconv_ref[...]
        conv = (w[0:1, :] * ubuf_ref[SUBLANES - 2:SUBLANES - 2 + tm, :]
                + w[1:2, :] * ubuf_ref[SUBLANES - 1:SUBLANES - 1 + tm, :]
                + w[2:3, :] * ubuf_ref[SUBLANES:, :])
        cin_ref[...] = (cb_ref[...].astype(F32) * conv).astype(BF16)

    y_attn = _dot(attn_ref[...], wa_ref[...].astype(BF16))
    y_conv = _dot(cin_ref[...], wc_ref[...].astype(BF16))
    o_ref[...] = (sa_ref[...].astype(F32) * y_attn + sc_ref[...].astype(F32) * y_conv).astype(BF16)


def _merge(attn2, z2, w_conv, w_attn_out, w_conv_out, *, tm, tn, seq):
    m = attn2.shape[0]
    cw = CONV_WIDTH
    halo_blocks = tm // SUBLANES
    vmem = (2 * tm * ATTN_WIDTH * 2 + 4 * tm * cw * 2 + (tm + SUBLANES) * cw * 4 + tm * cw * 2
            + 4 * tm * tn * 2 + 4 * ATTN_WIDTH * tn * 4 + 2 * ATTN_WIDTH * tn * 2
            + 2 * tm * tn * 2 + 3 * tm * tn * 4 + 3 * tm * cw * 4)
    return pl.pallas_call(
        functools.partial(_merge_kernel, seq // tm),
        out_shape=jax.ShapeDtypeStruct((m, D_MODEL), BF16),
        grid=(m // tm, D_MODEL // tn),
        in_specs=[
            pl.BlockSpec((tm, ATTN_WIDTH), lambda i, j: (i, 0)),
            pl.BlockSpec((tm, cw), lambda i, j: (i, Z_CB // cw)),
            pl.BlockSpec((tm, cw), lambda i, j: (i, Z_U // cw)),
            pl.BlockSpec((SUBLANES, cw), lambda i, j: (jnp.maximum(i * halo_blocks - 1, 0), Z_U // cw)),
            pl.BlockSpec((CONV_K, cw), lambda i, j: (0, 0)),
            pl.BlockSpec((tm, tn), lambda i, j: (i, Z_GA // tn + j)),
            pl.BlockSpec((tm, tn), lambda i, j: (i, Z_GC // tn + j)),
            pl.BlockSpec((ATTN_WIDTH, tn), lambda i, j: (0, j)),
            pl.BlockSpec((cw, tn), lambda i, j: (0, j)),
        ],
        out_specs=pl.BlockSpec((tm, tn), lambda i, j: (i, j)),
        scratch_shapes=[pltpu.VMEM((tm + SUBLANES, cw), F32), pltpu.VMEM((tm, cw), BF16)],
        compiler_params=pltpu.CompilerParams(
            dimension_semantics=("arbitrary", "arbitrary"), vmem_limit_bytes=vmem + 8 * MIB),
        name="merge",
    )(attn2, z2, z2, z2, w_conv, z2, z2, w_attn_out, w_conv_out)


def _out_proj_kernel(a_ref, w_ref, x_ref, o_ref):
    o_ref[...] = x_ref[...] + _dot(a_ref[...], w_ref[...].astype(BF16))


def _out_proj(merged, w_o, x2, *, tm, tn):
    m = merged.shape[0]
    vmem = (2 * tm * D_MODEL * 2 + 2 * D_MODEL * tn * 4 + D_MODEL * tn * 2 + 5 * tm * tn * 4)
    return pl.pallas_call(
        _out_proj_kernel,
        out_shape=jax.ShapeDtypeStruct((m, D_MODEL), F32),
        grid=(m // tm, D_MODEL // tn),
        in_specs=[
            pl.BlockSpec((tm, D_MODEL), lambda i, j: (i, 0)),
            pl.BlockSpec((D_MODEL, tn), lambda i, j: (0, j)),
            pl.BlockSpec((tm, tn), lambda i, j: (i, j)),
        ],
        out_specs=pl.BlockSpec((tm, tn), lambda i, j: (i, j)),
        compiler_params=pltpu.CompilerParams(
            dimension_semantics=("arbitrary", "arbitrary"), vmem_limit_bytes=vmem + 8 * MIB),
        name="out_proj",
    )(merged, w_o, x2)


def _mlp_kernel(r_ref, g_ref, wu_ref, wd_ref, o_ref, h_ref):
    f = pl.program_id(1)

    @pl.when(f == 0)
    def _():
        r = r_ref[...]
        h_ref[...] = (_rms_normalize(r) * g_ref[...]).astype(BF16)
        o_ref[...] = r

    a = jnp.maximum(_dot(h_ref[...], wu_ref[...]), 0.0)
    o_ref[...] += _dot((a * a).astype(BF16), wd_ref[...])


def _mlp(r, g_mlp, w_up, w_down, *, tm, tf):
    m = r.shape[0]
    vmem = (4 * tm * D_MODEL * 4 + tm * D_MODEL * 2 + 4 * D_MODEL * tf * 2 + 2 * tm * tf * 4)
    return pl.pallas_call(
        _mlp_kernel,
        out_shape=jax.ShapeDtypeStruct((m, D_MODEL), F32),
        grid=(m // tm, D_FF // tf),
        in_specs=[
            pl.BlockSpec((tm, D_MODEL), lambda i, f: (i, 0)),
            pl.BlockSpec((1, D_MODEL), lambda i, f: (0, 0)),
            pl.BlockSpec((D_MODEL, tf), lambda i, f: (0, f)),
            pl.BlockSpec((tf, D_MODEL), lambda i, f: (f, 0)),
        ],
        out_specs=pl.BlockSpec((tm, D_MODEL), lambda i, f: (i, 0)),
        scratch_shapes=[pltpu.VMEM((tm, D_MODEL), BF16)],
        compiler_params=pltpu.CompilerParams(
            dimension_semantics=("arbitrary", "arbitrary"), vmem_limit_bytes=vmem + 8 * MIB),
        name="mlp",
    )(r, g_mlp, w_up, w_down)


def _ple_kernel(r_ref, g_ref, p_ref, wg_ref, wp_ref, o_ref, h_ref, pb_ref):
    j = pl.program_id(1)
    tn = o_ref.shape[1]

    @pl.when(j == 0)
    def _():
        h_ref[...] = (_rms_normalize(r_ref[...]) * g_ref[...]).astype(BF16)
        pb_ref[...] = p_ref[...].astype(BF16)

    gate = jax.nn.sigmoid(_dot(h_ref[...], wg_ref[...].astype(BF16)))
    proj = _dot(pb_ref[...], wp_ref[...].astype(BF16))
    col = pl.multiple_of(j * tn, tn)
    o_ref[...] = r_ref[:, pl.ds(col, tn)] + gate * proj


def _ple(r, g_ple, p2, w_gate, w_proj, *, tm, tn):
    m = r.shape[0]
    vmem = (2 * tm * D_MODEL * 4 + tm * D_MODEL * 2 + 2 * tm * PLE_DIM * 4 + tm * PLE_DIM * 2
            + 2 * D_MODEL * tn * 4 + D_MODEL * tn * 2 + 2 * PLE_DIM * tn * 4 + 5 * tm * tn * 4)
    return pl.pallas_call(
        _ple_kernel,
        out_shape=jax.ShapeDtypeStruct((m, D_MODEL), F32),
        grid=(m // tm, D_MODEL // tn),
        in_specs=[
            pl.BlockSpec((tm, D_MODEL), lambda i, j: (i, 0)),
            pl.BlockSpec((1, D_MODEL), lambda i, j: (0, 0)),
            pl.BlockSpec((tm, PLE_DIM), lambda i, j: (i, 0)),
            pl.BlockSpec((D_MODEL, tn), lambda i, j: (0, j)),
            pl.BlockSpec((PLE_DIM, tn), lambda i, j: (0, j)),
        ],
        out_specs=pl.BlockSpec((tm, tn), lambda i, j: (i, j)),
        scratch_shapes=[pltpu.VMEM((tm, D_MODEL), BF16), pltpu.VMEM((tm, PLE_DIM), BF16)],
        compiler_params=pltpu.CompilerParams(
            dimension_semantics=("arbitrary", "arbitrary"), vmem_limit_bytes=vmem + 8 * MIB),
        name="ple",
    )(r, g_ple, p2, w_gate, w_proj)


def kernel(x, p, g_mix, w_in, g_q, g_k, w_conv, w_attn_out, w_conv_out, w_o, g_mlp, w_up, w_down,
           g_ple, w_ple_gate, w_ple_proj):
    b, s, d = x.shape
    depth = p.shape[0]
    m = b * s
    r = x.reshape(m, d)
    for i in range(depth):
        z = _in_proj(r, g_mix[i][None], g_q[i][None], g_k[i][None], w_in[i], tm=1024)
        attn = _moba(z.reshape(b, s, Z_COLS)).reshape(m, ATTN_WIDTH)
        merged = _merge(attn, z, w_conv[i], w_attn_out[i], w_conv_out[i], tm=1024, tn=512, seq=s)
        r = _out_proj(merged, w_o[i], r, tm=1024, tn=512)
        r = _mlp(r, g_mlp[i][None], w_up[i].astype(BF16), w_down[i].astype(BF16), tm=512, tf=1024)
        r = _ple(r, g_ple[i][None], p[i].reshape(m, PLE_DIM), w_ple_gate[i], w_ple_proj[i],
                 tm=1024, tn=512)
    return r.reshape(b, s, d)
```

```python
import functools
import math

import jax
import jax.numpy as jnp
from jax import lax
from jax.experimental import pallas as pl
from jax.experimental.pallas import tpu as pltpu

F32 = jnp.float32
BF16 = jnp.bfloat16

D_MODEL = 2048
N_HEADS = 8
HEAD_DIM = 128
ATTN_WIDTH = N_HEADS * HEAD_DIM
CONV_WIDTH = 1024
CONV_K = 3
MOBA_BLOCK = 256
MOBA_TOP_K = 3
D_FF = 4 * D_MODEL
PLE_DIM = 256
RMS_EPS = 1e-6
IN_COLS = 3 * ATTN_WIDTH + 3 * CONV_WIDTH + 2 * D_MODEL
Q_PRESCALE = HEAD_DIM ** -0.5 * math.log2(math.e)

LANES = 128
SUBLANES = 8
MIB = 1024 * 1024

IN_TN = 1024
IN_SUB = 512
_STEP_Q, _STEP_K, _STEP_V, _STEP_CB, _STEP_CC, _STEP_CX, _STEP_GATES = 0, 1, 2, 3, 4, 5, 6
_IN_STEPS = IN_COLS // IN_TN
MIX_SUB = 512
Z_COLS = IN_COLS - CONV_WIDTH
Z_GA, Z_GC = 0, D_MODEL
Z_Q = 2 * D_MODEL
Z_K = Z_Q + ATTN_WIDTH
Z_V = Z_K + ATTN_WIDTH
Z_CB = Z_V + ATTN_WIDTH
Z_U = Z_CB + CONV_WIDTH


def _rms_normalize(x):
    return x * lax.rsqrt(jnp.mean(x * x, axis=-1, keepdims=True) + RMS_EPS)


def _dot(a, b):
    return jnp.dot(a, b, preferred_element_type=F32)


def _dot_nt(a, b):
    return lax.dot_general(a, b, (((1,), (1,)), ((), ())), preferred_element_type=F32)


def _in_proj_kernel(x_ref, gmix_ref, gq_ref, gk_ref, w_ref, z_ref, h_ref, cc_ref):
    j = pl.program_id(1)

    @pl.when(j == 0)
    def _():
        h_ref[...] = (_rms_normalize(x_ref[...]) * gmix_ref[...]).astype(BF16)

    def for_each_sub(epilogue):
        for c in range(IN_TN // IN_SUB):
            sl = slice(c * IN_SUB, (c + 1) * IN_SUB)
            epilogue(sl, _dot(h_ref[...], w_ref[:, sl].astype(BF16)))

    def head_norm(gain):
        def epilogue(sl, acc):
            for c in range(IN_SUB // HEAD_DIM):
                hs = slice(c * HEAD_DIM, (c + 1) * HEAD_DIM)
                z_ref[:, sl.start + hs.start:sl.start + hs.stop] = (
                    _rms_normalize(acc[:, hs]) * gain).astype(BF16)
        return epilogue

    @pl.when(j == _STEP_Q)
    def _():
        for_each_sub(head_norm(gq_ref[...] * Q_PRESCALE))

    @pl.when(j == _STEP_K)
    def _():
        for_each_sub(head_norm(gk_ref[...]))

    @pl.when((j == _STEP_V) | (j == _STEP_CB))
    def _():
        def epilogue(sl, acc):
            z_ref[:, sl] = acc.astype(BF16)
        for_each_sub(epilogue)

    @pl.when(j == _STEP_CC)
    def _():
        def epilogue(sl, acc):
            cc_ref[:, sl] = acc
        for_each_sub(epilogue)

    @pl.when(j == _STEP_CX)
    def _():
        def epilogue(sl, acc):
            z_ref[:, sl] = (cc_ref[:, sl] * acc).astype(BF16)
        for_each_sub(epilogue)

    @pl.when(j >= _STEP_GATES)
    def _():
        def epilogue(sl, acc):
            z_ref[:, sl] = jax.nn.sigmoid(acc).astype(BF16)
        for_each_sub(epilogue)


def _in_proj(x2, g_mix, g_q, g_k, w_in, *, tm):
    m = x2.shape[0]

    def z_index(i, j):
        attn_conv = Z_Q // IN_TN + jnp.where(j < _STEP_CC, j, j - 1)
        return i, jnp.where(j < _STEP_GATES, attn_conv, j - _STEP_GATES)

    vmem = (2 * tm * D_MODEL * 4 + tm * D_MODEL * 2 + 2 * D_MODEL * IN_TN * 4 + D_MODEL * IN_SUB * 2
            + 2 * tm * IN_TN * 2 + tm * IN_TN * 4 + 2 * tm * IN_SUB * 4)
    return pl.pallas_call(
        _in_proj_kernel,
        out_shape=jax.ShapeDtypeStruct((m, Z_COLS), BF16),
        grid=(m // tm, _IN_STEPS),
        in_specs=[
            pl.BlockSpec((tm, D_MODEL), lambda i, j: (i, 0)),
            pl.BlockSpec((1, D_MODEL), lambda i, j: (0, 0)),
            pl.BlockSpec((1, HEAD_DIM), lambda i, j: (0, 0)),
            pl.BlockSpec((1, HEAD_DIM), lambda i, j: (0, 0)),
            pl.BlockSpec((D_MODEL, IN_TN), lambda i, j: (0, j)),
        ],
        out_specs=pl.BlockSpec((tm, IN_TN), z_index),
        scratch_shapes=[pltpu.VMEM((tm, D_MODEL), BF16), pltpu.VMEM((tm, IN_TN), F32)],
        compiler_params=pltpu.CompilerParams(
            dimension_semantics=("arbitrary", "arbitrary"), vmem_limit_bytes=vmem + 6 * MIB),
        name="in_proj",
    )(x2, g_mix, g_q, g_k, w_in)


def _moba_kernel(q_ref, k_ref, v_ref, o_ref, kmean_ref, vt_ref, bias_ref, m_ref, l_ref, acc_ref):
    j = pl.program_id(1)
    nb = k_ref.shape[1] // MOBA_BLOCK
    heads = k_ref.shape[2] // HEAD_DIM
    neg_inf = F32(-jnp.inf)

    def head_cols(h):
        return slice(h * HEAD_DIM, (h + 1) * HEAD_DIM)

    @pl.when(j == 0)
    def _():
        for h in range(heads):
            for n in range(nb):
                blk = slice(n * MOBA_BLOCK, (n + 1) * MOBA_BLOCK)
                kmean_ref[h, n:n + 1, :] = jnp.mean(
                    k_ref[0, blk, head_cols(h)].astype(F32), axis=0, keepdims=True)
                vt_ref[head_cols(h), blk] = v_ref[0, blk, head_cols(h)].T

    for h in range(heads):
        q = q_ref[0, :, head_cols(h)]
        kmean = kmean_ref[h]
        km_hi = kmean.astype(BF16)
        km_lo = (kmean - km_hi.astype(F32)).astype(BF16)
        gate = _dot_nt(km_hi, q) + _dot_nt(km_lo, q)
        blk_id = lax.broadcasted_iota(jnp.int32, gate.shape, 0)
        gate = jnp.where(blk_id < j, gate, neg_inf)
        sel = jnp.zeros(gate.shape, jnp.bool_)
        for t in range(MOBA_TOP_K):
            best = jnp.max(gate, axis=0, keepdims=True)
            first = jnp.min(jnp.where(gate == best, blk_id, nb), axis=0, keepdims=True)
            pick = blk_id == first
            sel = sel | (pick & (t < j))
            gate = jnp.where(pick, neg_inf, gate)
        bias_ref[h] = jnp.where(sel, F32(0), neg_inf)

    def attend(n, own):
        off = pl.multiple_of(n * MOBA_BLOCK, MOBA_BLOCK)
        scores = [_dot_nt(k_ref[0, pl.ds(off, MOBA_BLOCK), head_cols(h)], q_ref[0, :, head_cols(h)])
                  for h in range(heads)]
        probs, alphas = [], []
        for h in range(heads):
            s = scores[h]
            if own:
                key_i = lax.broadcasted_iota(jnp.int32, s.shape, 0)
                qry_i = lax.broadcasted_iota(jnp.int32, s.shape, 1)
                s = jnp.where(key_i <= qry_i, s, neg_inf)
                m_new = jnp.max(s, axis=0, keepdims=True)
                shift, alpha = m_new, None
            else:
                bias = bias_ref[h, pl.ds(n, 1), :]
                m_old = m_ref[h]
                m_new = jnp.maximum(m_old, jnp.max(s, axis=0, keepdims=True) + bias)
                alpha = jnp.exp2(m_old - m_new)
                shift = m_new - bias
            p = jnp.exp2(s - shift)
            l_blk = jnp.sum(p, axis=0, keepdims=True)
            m_ref[h] = m_new
            l_ref[h] = l_blk if own else alpha * l_ref[h] + l_blk
            probs.append(p.astype(BF16))
            alphas.append(alpha)
        for h in range(heads):
            pv = _dot(vt_ref[head_cols(h), pl.ds(off, MOBA_BLOCK)], probs[h])
            acc_ref[h] = pv if own else alphas[h] * acc_ref[h] + pv

    attend(j, True)

    def body(n, carry):
        attend(n, False)
        return carry

    lax.fori_loop(0, j, body, 0)
    for h in range(heads):
        o_ref[0, :, head_cols(h)] = (acc_ref[h] / l_ref[h]).T.astype(BF16)


def _moba(z3):
    b, s, _ = z3.shape
    nq = s // MOBA_BLOCK
    aw = ATTN_WIDTH
    vmem = (4 * s * aw * 2 + s * aw * 2 + 4 * MOBA_BLOCK * aw * 2
            + N_HEADS * (HEAD_DIM + 2 * nq + 2 * SUBLANES) * MOBA_BLOCK * 4)
    return pl.pallas_call(
        _moba_kernel,
        out_shape=jax.ShapeDtypeStruct((b, s, aw), BF16),
        grid=(b, nq),
        in_specs=[
            pl.BlockSpec((1, MOBA_BLOCK, aw), lambda bi, j: (bi, j, Z_Q // aw)),
            pl.BlockSpec((1, s, aw), lambda bi, j: (bi, 0, Z_K // aw)),
            pl.BlockSpec((1, s, aw), lambda bi, j: (bi, 0, Z_V // aw)),
        ],
        out_specs=pl.BlockSpec((1, MOBA_BLOCK, aw), lambda bi, j: (bi, j, 0)),
        scratch_shapes=[
            pltpu.VMEM((N_HEADS, nq, HEAD_DIM), F32),
            pltpu.VMEM((aw, s), BF16),
            pltpu.VMEM((N_HEADS, nq, MOBA_BLOCK), F32),
            pltpu.VMEM((N_HEADS, 1, MOBA_BLOCK), F32),
            pltpu.VMEM((N_HEADS, 1, MOBA_BLOCK), F32),
            pltpu.VMEM((N_HEADS, HEAD_DIM, MOBA_BLOCK), F32),
        ],
        compiler_params=pltpu.CompilerParams(
            dimension_semantics=("arbitrary", "arbitrary"), vmem_limit_bytes=vmem + 6 * MIB),
        name="moba",
    )(z3, z3, z3)


def _mix_kernel(seq_tiles, attn_ref, cb_ref, u_ref, halo_ref, wconv_ref, sa_ref, sc_ref, x_ref,
                wa_ref, wc_ref, wo_ref, o_ref, ubuf_ref, cin_ref, mg_ref):
    i = pl.program_id(0)
    tm = attn_ref.shape[0]

    halo = halo_ref[...].astype(F32)
    ubuf_ref[0:SUBLANES, :] = jnp.where(i % seq_tiles == 0, jnp.zeros_like(halo), halo)
    ubuf_ref[SUBLANES:, :] = u_ref[...].astype(F32)
    w = wconv_ref[...]
    conv = (w[0:1, :] * ubuf_ref[SUBLANES - 2:SUBLANES - 2 + tm, :]
            + w[1:2, :] * ubuf_ref[SUBLANES - 1:SUBLANES - 1 + tm, :]
            + w[2:3, :] * ubuf_ref[SUBLANES:, :])
    cin_ref[...] = (cb_ref[...].astype(F32) * conv).astype(BF16)

    for c in range(D_MODEL // MIX_SUB):
        sl = slice(c * MIX_SUB, (c + 1) * MIX_SUB)
        y_attn = _dot(attn_ref[...], wa_ref[:, sl])
        y_conv = _dot(cin_ref[...], wc_ref[:, sl])
        mg_ref[:, sl] = (sa_ref[:, sl].astype(F32) * y_attn
                         + sc_ref[:, sl].astype(F32) * y_conv).astype(BF16)
    for c in range(D_MODEL // MIX_SUB):
        sl = slice(c * MIX_SUB, (c + 1) * MIX_SUB)
        o_ref[:, sl] = x_ref[:, sl] + _dot(mg_ref[...], wo_ref[:, sl])


def _resident(shape):
    return pl.BlockSpec(shape, lambda i: (0,) * len(shape), pipeline_mode=pl.Buffered(1))


def _mix(attn2, z2, x2, w_conv, w_attn_out, w_conv_out, w_o, *, tm, seq):
    m = attn2.shape[0]
    cw, d = CONV_WIDTH, D_MODEL
    halo_blocks = tm // SUBLANES
    vmem = (2 * tm * ATTN_WIDTH * 2 + 4 * tm * cw * 2 + 4 * tm * d * 2 + 4 * tm * d * 4
            + (tm + SUBLANES) * cw * 4 + tm * cw * 2 + tm * d * 2
            + (ATTN_WIDTH + cw + d) * d * 2 + 4 * tm * MIX_SUB * 4)
    return pl.pallas_call(
        functools.partial(_mix_kernel, seq // tm),
        out_shape=jax.ShapeDtypeStruct((m, d), F32),
        grid=(m // tm,),
        in_specs=[
            pl.BlockSpec((tm, ATTN_WIDTH), lambda i: (i, 0)),
            pl.BlockSpec((tm, cw), lambda i: (i, Z_CB // cw)),
            pl.BlockSpec((tm, cw), lambda i: (i, Z_U // cw)),
            pl.BlockSpec((SUBLANES, cw), lambda i: (jnp.maximum(i * halo_blocks - 1, 0), Z_U // cw)),
            pl.BlockSpec((CONV_K, cw), lambda i: (0, 0)),
            pl.BlockSpec((tm, d), lambda i: (i, Z_GA // d)),
            pl.BlockSpec((tm, d), lambda i: (i, Z_GC // d)),
            pl.BlockSpec((tm, d), lambda i: (i, 0)),
            _resident((ATTN_WIDTH, d)),
            _resident((cw, d)),
            _resident((d, d)),
        ],
        out_specs=pl.BlockSpec((tm, d), lambda i: (i, 0)),
        scratch_shapes=[pltpu.VMEM((tm + SUBLANES, cw), F32), pltpu.VMEM((tm, cw), BF16),
                        pltpu.VMEM((tm, d), BF16)],
        compiler_params=pltpu.CompilerParams(
            dimension_semantics=("arbitrary",), vmem_limit_bytes=vmem + 6 * MIB),
        name="mix",
    )(attn2, z2, z2, z2, w_conv, z2, z2, x2, w_attn_out, w_conv_out, w_o)


def _mlp_kernel(r_ref, g_ref, wu_ref, wd_ref, o_ref, h_ref):
    f = pl.program_id(1)

    @pl.when(f == 0)
    def _():
        r = r_ref[...]
        h_ref[...] = (_rms_normalize(r) * g_ref[...]).astype(BF16)
        o_ref[...] = r

    a = jnp.maximum(_dot(h_ref[...], wu_ref[...]), 0.0)
    o_ref[...] += _dot((a * a).astype(BF16), wd_ref[...])


def _mlp(r, g_mlp, w_up, w_down, *, tm, tf):
    m = r.shape[0]
    vmem = (4 * tm * D_MODEL * 4 + tm * D_MODEL * 2 + 4 * D_MODEL * tf * 2 + 2 * tm * tf * 4)
    return pl.pallas_call(
        _mlp_kernel,
        out_shape=jax.ShapeDtypeStruct((m, D_MODEL), F32),
        grid=(m // tm, D_FF // tf),
        in_specs=[
            pl.BlockSpec((tm, D_MODEL), lambda i, f: (i, 0)),
            pl.BlockSpec((1, D_MODEL), lambda i, f: (0, 0)),
            pl.BlockSpec((D_MODEL, tf), lambda i, f: (0, f)),
            pl.BlockSpec((tf, D_MODEL), lambda i, f: (f, 0)),
        ],
        out_specs=pl.BlockSpec((tm, D_MODEL), lambda i, f: (i, 0)),
        scratch_shapes=[pltpu.VMEM((tm, D_MODEL), BF16)],
        compiler_params=pltpu.CompilerParams(
            dimension_semantics=("arbitrary", "arbitrary"), vmem_limit_bytes=vmem + 8 * MIB),
        name="mlp",
    )(r, g_mlp, w_up, w_down)


def _ple_kernel(r_ref, g_ref, p_ref, wg_ref, wp_ref, o_ref, h_ref):
    h_ref[...] = (_rms_normalize(r_ref[...]) * g_ref[...]).astype(BF16)
    pb = p_ref[...].astype(BF16)
    for c in range(D_MODEL // MIX_SUB):
        sl = slice(c * MIX_SUB, (c + 1) * MIX_SUB)
        gate = jax.nn.sigmoid(_dot(h_ref[...], wg_ref[:, sl]))
        o_ref[:, sl] = r_ref[:, sl] + gate * _dot(pb, wp_ref[:, sl])


def _ple(r, g_ple, p2, w_gate, w_proj, *, tm):
    m = r.shape[0]
    d = D_MODEL
    vmem = (4 * tm * d * 4 + tm * d * 2 + 2 * tm * PLE_DIM * 4 + tm * PLE_DIM * 2
            + (d + PLE_DIM) * d * 2 + 4 * tm * MIX_SUB * 4)
    return pl.pallas_call(
        _ple_kernel,
        out_shape=jax.ShapeDtypeStruct((m, d), F32),
        grid=(m // tm,),
        in_specs=[
            pl.BlockSpec((tm, d), lambda i: (i, 0)),
            pl.BlockSpec((1, d), lambda i: (0, 0)),
            pl.BlockSpec((tm, PLE_DIM), lambda i: (i, 0)),
            _resident((d, d)),
            _resident((PLE_DIM, d)),
        ],
        out_specs=pl.BlockSpec((tm, d), lambda i: (i, 0)),
        scratch_shapes=[pltpu.VMEM((tm, d), BF16)],
        compiler_params=pltpu.CompilerParams(
            dimension_semantics=("arbitrary",), vmem_limit_bytes=vmem + 6 * MIB),
        name="ple",
    )(r, g_ple, p2, w_gate, w_proj)


def kernel(x, p, g_mix, w_in, g_q, g_k, w_conv, w_attn_out, w_conv_out, w_o, g_mlp, w_up, w_down,
           g_ple, w_ple_gate, w_ple_proj):
    b, s, d = x.shape
    depth = p.shape[0]
    m = b * s
    r = x.reshape(m, d)
    for i in range(depth):
        z = _in_proj(r, g_mix[i][None], g_q[i][None], g_k[i][None], w_in[i], tm=1024)
        attn = _moba(z.reshape(b, s, Z_COLS)).reshape(m, ATTN_WIDTH)
        r = _mix(attn, z, r, w_conv[i], w_attn_out[i].astype(BF16), w_conv_out[i].astype(BF16),
                 w_o[i].astype(BF16), tm=512, seq=s)
        r = _mlp(r, g_mlp[i][None], w_up[i].astype(BF16), w_down[i].astype(BF16), tm=512, tf=1024)
        r = _ple(r, g_ple[i][None], p[i].reshape(m, PLE_DIM), w_ple_gate[i].astype(BF16),
                 w_ple_proj[i].astype(BF16), tm=512)
    return r.reshape(b, s, d)
```

```python
import functools
import math

import jax
import jax.numpy as jnp
from jax import lax
from jax.experimental import pallas as pl
from jax.experimental.pallas import tpu as pltpu

F32 = jnp.float32
BF16 = jnp.bfloat16

D_MODEL = 2048
N_HEADS = 8
HEAD_DIM = 128
ATTN_WIDTH = N_HEADS * HEAD_DIM
CONV_WIDTH = 1024
CONV_K = 3
MOBA_BLOCK = 256
MOBA_TOP_K = 3
D_FF = 4 * D_MODEL
PLE_DIM = 256
RMS_EPS = 1e-6
IN_COLS = 3 * ATTN_WIDTH + 3 * CONV_WIDTH + 2 * D_MODEL
Q_PRESCALE = HEAD_DIM ** -0.5 * math.log2(math.e)

LANES = 128
SUBLANES = 8
MIB = 1024 * 1024

IN_TN = 1024
IN_SPLITS = (0, 768, 1024)
_STEP_Q, _STEP_K, _STEP_V, _STEP_CB, _STEP_CC, _STEP_CX, _STEP_GATES = 0, 1, 2, 3, 4, 5, 6
_IN_STEPS = IN_COLS // IN_TN
MIX_SUB = 512
Z_COLS = IN_COLS - CONV_WIDTH
Z_GA, Z_GC = 0, D_MODEL
Z_Q = 2 * D_MODEL
Z_K = Z_Q + ATTN_WIDTH
Z_V = Z_K + ATTN_WIDTH
Z_CB = Z_V + ATTN_WIDTH
Z_U = Z_CB + CONV_WIDTH


def _rms_normalize(x):
    return x * lax.rsqrt(jnp.mean(x * x, axis=-1, keepdims=True) + RMS_EPS)


def _sigmoid(x):
    return 0.5 * jnp.tanh(0.5 * x) + 0.5


def _dot(a, b):
    return jnp.dot(a, b, preferred_element_type=F32)


def _dot_nt(a, b):
    return lax.dot_general(a, b, (((1,), (1,)), ((), ())), preferred_element_type=F32)


def _in_proj_kernel(x_ref, gmix_ref, gq_ref, gk_ref, w_ref, z_ref, h_ref, cc_ref):
    j = pl.program_id(1)

    @pl.when(j == 0)
    def _():
        h_ref[...] = (_rms_normalize(x_ref[...]) * gmix_ref[...]).astype(BF16)

    def for_each_sub(epilogue):
        for lo, hi in zip(IN_SPLITS[:-1], IN_SPLITS[1:]):
            sl = slice(lo, hi)
            epilogue(sl, _dot(h_ref[...], w_ref[:, sl].astype(BF16)))

    def head_norm(gain):
        def epilogue(sl, acc):
            for c in range((sl.stop - sl.start) // HEAD_DIM):
                hs = slice(c * HEAD_DIM, (c + 1) * HEAD_DIM)
                z_ref[:, sl.start + hs.start:sl.start + hs.stop] = (
                    _rms_normalize(acc[:, hs]) * gain).astype(BF16)
        return epilogue

    @pl.when(j == _STEP_Q)
    def _():
        for_each_sub(head_norm(gq_ref[...] * Q_PRESCALE))

    @pl.when(j == _STEP_K)
    def _():
        for_each_sub(head_norm(gk_ref[...]))

    @pl.when((j == _STEP_V) | (j == _STEP_CB))
    def _():
        def epilogue(sl, acc):
            z_ref[:, sl] = acc.astype(BF16)
        for_each_sub(epilogue)

    @pl.when(j == _STEP_CC)
    def _():
        def epilogue(sl, acc):
            cc_ref[:, sl] = acc
        for_each_sub(epilogue)

    @pl.when(j == _STEP_CX)
    def _():
        def epilogue(sl, acc):
            z_ref[:, sl] = (cc_ref[:, sl] * acc).astype(BF16)
        for_each_sub(epilogue)

    @pl.when(j >= _STEP_GATES)
    def _():
        def epilogue(sl, acc):
            z_ref[:, sl] = _sigmoid(acc).astype(BF16)
        for_each_sub(epilogue)


def _in_proj(x2, g_mix, g_q, g_k, w_in, *, tm):
    m = x2.shape[0]

    def z_index(i, j):
        attn_conv = Z_Q // IN_TN + jnp.where(j < _STEP_CC, j, j - 1)
        return i, jnp.where(j < _STEP_GATES, attn_conv, j - _STEP_GATES)

    vmem = (2 * tm * D_MODEL * 4 + tm * D_MODEL * 2 + 2 * D_MODEL * IN_TN * 4 + D_MODEL * IN_TN * 2
            + 2 * tm * IN_TN * 2 + tm * IN_TN * 4 + tm * IN_TN * 4)
    return pl.pallas_call(
        _in_proj_kernel,
        out_shape=jax.ShapeDtypeStruct((m, Z_COLS), BF16),
        grid=(m // tm, _IN_STEPS),
        in_specs=[
            pl.BlockSpec((tm, D_MODEL), lambda i, j: (i, 0)),
            pl.BlockSpec((1, D_MODEL), lambda i, j: (0, 0)),
            pl.BlockSpec((1, HEAD_DIM), lambda i, j: (0, 0)),
            pl.BlockSpec((1, HEAD_DIM), lambda i, j: (0, 0)),
            pl.BlockSpec((D_MODEL, IN_TN), lambda i, j: (0, j)),
        ],
        out_specs=pl.BlockSpec((tm, IN_TN), z_index),
        scratch_shapes=[pltpu.VMEM((tm, D_MODEL), BF16), pltpu.VMEM((tm, IN_TN), F32)],
        compiler_params=pltpu.CompilerParams(
            dimension_semantics=("arbitrary", "arbitrary"), vmem_limit_bytes=vmem + 6 * MIB),
        name="in_proj",
    )(x2, g_mix, g_q, g_k, w_in)


def _moba_kernel(q_ref, k_ref, v_ref, o_ref, kmean_ref, vt_ref, bias_ref, m_ref, l_ref, acc_ref,
                 sa_ref, sb_ref):
    j = pl.program_id(1)
    nb = k_ref.shape[1] // MOBA_BLOCK
    heads = k_ref.shape[2] // HEAD_DIM
    neg_inf = F32(-jnp.inf)

    def head_cols(h):
        return slice(h * HEAD_DIM, (h + 1) * HEAD_DIM)

    @pl.when(j == 0)
    def _():
        for h in range(heads):
            for n in range(nb):
                blk = slice(n * MOBA_BLOCK, (n + 1) * MOBA_BLOCK)
                kmean_ref[h, n:n + 1, :] = jnp.mean(
                    k_ref[0, blk, head_cols(h)].astype(F32), axis=0, keepdims=True)
                vt_ref[head_cols(h), blk] = v_ref[0, blk, head_cols(h)].T

    for h in range(heads):
        q = q_ref[0, :, head_cols(h)]
        kmean = kmean_ref[h]
        km_hi = kmean.astype(BF16)
        km_lo = (kmean - km_hi.astype(F32)).astype(BF16)
        gate = _dot_nt(km_hi, q) + _dot_nt(km_lo, q)
        blk_id = lax.broadcasted_iota(jnp.int32, gate.shape, 0)
        gate = jnp.where(blk_id < j, gate, neg_inf)
        sel = jnp.zeros(gate.shape, jnp.bool_)
        for t in range(MOBA_TOP_K):
            best = jnp.max(gate, axis=0, keepdims=True)
            first = jnp.min(jnp.where(gate == best, blk_id, nb), axis=0, keepdims=True)
            pick = blk_id == first
            sel = sel | (pick & (t < j))
            gate = jnp.where(pick, neg_inf, gate)
        bias_ref[h] = jnp.where(sel, F32(0), neg_inf)

    def scores(n, h):
        off = pl.multiple_of(n * MOBA_BLOCK, MOBA_BLOCK)
        return _dot_nt(k_ref[0, pl.ds(off, MOBA_BLOCK), head_cols(h)], q_ref[0, :, head_cols(h)])

    def attend(n, cur_ref, nxt_ref, n_next, own):
        off = pl.multiple_of(n * MOBA_BLOCK, MOBA_BLOCK)
        if nxt_ref is not None:
            n_next = jnp.clip(n_next, 0, jnp.maximum(j - 1, 0))
        for h in range(heads):
            if nxt_ref is not None:
                nxt_ref[h] = scores(n_next, h)
            s = cur_ref[h]
            if own:
                key_i = lax.broadcasted_iota(jnp.int32, s.shape, 0)
                qry_i = lax.broadcasted_iota(jnp.int32, s.shape, 1)
                s = jnp.where(key_i <= qry_i, s, neg_inf)
                m_new = jnp.max(s, axis=0, keepdims=True)
                shift, alpha = m_new, None
            else:
                bias = bias_ref[h, pl.ds(n, 1), :]
                m_old = m_ref[h]
                m_new = jnp.maximum(m_old, jnp.max(s, axis=0, keepdims=True) + bias)
                alpha = jnp.exp2(m_old - m_new)
                shift = m_new - bias
            p = jnp.exp2(s - shift)
            l_blk = jnp.sum(p, axis=0, keepdims=True)
            m_ref[h] = m_new
            l_ref[h] = l_blk if own else alpha * l_ref[h] + l_blk
            pv = _dot(vt_ref[head_cols(h), pl.ds(off, MOBA_BLOCK)], p.astype(BF16))
            acc_ref[h] = pv if own else alpha * acc_ref[h] + pv

    for h in range(heads):
        sa_ref[h] = scores(j, h)
    attend(j, sa_ref, sb_ref, 0, True)

    def body(i, carry):
        attend(2 * i, sb_ref, sa_ref, 2 * i + 1, False)
        attend(2 * i + 1, sa_ref, sb_ref, 2 * i + 2, False)
        return carry

    lax.fori_loop(0, j // 2, body, 0)

    @pl.when(j % 2 == 1)
    def _():
        attend(j - 1, sb_ref, None, None, False)

    for h in range(heads):
        o_ref[0, :, head_cols(h)] = (acc_ref[h] / l_ref[h]).T.astype(BF16)


def _moba(z3):
    b, s, _ = z3.shape
    nq = s // MOBA_BLOCK
    aw = ATTN_WIDTH
    vmem = (4 * s * aw * 2 + s * aw * 2 + 4 * MOBA_BLOCK * aw * 2
            + N_HEADS * (HEAD_DIM + 2 * nq + 2 * SUBLANES + 2 * MOBA_BLOCK) * MOBA_BLOCK * 4)
    return pl.pallas_call(
        _moba_kernel,
        out_shape=jax.ShapeDtypeStruct((b, s, aw), BF16),
        grid=(b, nq),
        in_specs=[
            pl.BlockSpec((1, MOBA_BLOCK, aw), lambda bi, j: (bi, j, Z_Q // aw)),
            pl.BlockSpec((1, s, aw), lambda bi, j: (bi, 0, Z_K // aw)),
            pl.BlockSpec((1, s, aw), lambda bi, j: (bi, 0, Z_V // aw)),
        ],
        out_specs=pl.BlockSpec((1, MOBA_BLOCK, aw), lambda bi, j: (bi, j, 0)),
        scratch_shapes=[
            pltpu.VMEM((N_HEADS, nq, HEAD_DIM), F32),
            pltpu.VMEM((aw, s), BF16),
            pltpu.VMEM((N_HEADS, nq, MOBA_BLOCK), F32),
            pltpu.VMEM((N_HEADS, 1, MOBA_BLOCK), F32),
            pltpu.VMEM((N_HEADS, 1, MOBA_BLOCK), F32),
            pltpu.VMEM((N_HEADS, HEAD_DIM, MOBA_BLOCK), F32),
            pltpu.VMEM((N_HEADS, MOBA_BLOCK, MOBA_BLOCK), F32),
            pltpu.VMEM((N_HEADS, MOBA_BLOCK, MOBA_BLOCK), F32),
        ],
        compiler_params=pltpu.CompilerParams(
            dimension_semantics=("arbitrary", "arbitrary"), vmem_limit_bytes=vmem + 6 * MIB),
        name="moba",
    )(z3, z3, z3)


def _mix_kernel(seq_tiles, attn_ref, cb_ref, u_ref, halo_ref, wconv_ref, sa_ref, sc_ref, x_ref,
                wa_ref, wc_ref, wo_ref, o_ref, ubuf_ref, cin_ref, mg_ref):
    i = pl.program_id(0)
    tm = attn_ref.shape[0]

    halo = halo_ref[...].astype(F32)
    ubuf_ref[0:SUBLANES, :] = jnp.where(i % seq_tiles == 0, jnp.zeros_like(halo), halo)
    ubuf_ref[SUBLANES:, :] = u_ref[...].astype(F32)
    w = wconv_ref[...]
    conv = (w[0:1, :] * ubuf_ref[SUBLANES - 2:SUBLANES - 2 + tm, :]
            + w[1:2, :] * ubuf_ref[SUBLANES - 1:SUBLANES - 1 + tm, :]
            + w[2:3, :] * ubuf_ref[SUBLANES:, :])
    cin_ref[...] = (cb_ref[...].astype(F32) * conv).astype(BF16)

    for c in range(D_MODEL // MIX_SUB):
        sl = slice(c * MIX_SUB, (c + 1) * MIX_SUB)
        y_attn = _dot(attn_ref[...], wa_ref[:, sl])
        y_conv = _dot(cin_ref[...], wc_ref[:, sl])
        mg_ref[:, sl] = (sa_ref[:, sl].astype(F32) * y_attn
                         + sc_ref[:, sl].astype(F32) * y_conv).astype(BF16)
    for c in range(D_MODEL // MIX_SUB):
        sl = slice(c * MIX_SUB, (c + 1) * MIX_SUB)
        o_ref[:, sl] = x_ref[:, sl] + _dot(mg_ref[...], wo_ref[:, sl])


def _resident(shape):
    return pl.BlockSpec(shape, lambda i: (0,) * len(shape), pipeline_mode=pl.Buffered(1))


def _mix(attn2, z2, x2, w_conv, w_attn_out, w_conv_out, w_o, *, tm, seq):
    m = attn2.shape[0]
    cw, d = CONV_WIDTH, D_MODEL
    halo_blocks = tm // SUBLANES
    vmem = (2 * tm * ATTN_WIDTH * 2 + 4 * tm * cw * 2 + 4 * tm * d * 2 + 4 * tm * d * 4
            + (tm + SUBLANES) * cw * 4 + tm * cw * 2 + tm * d * 2
            + (ATTN_WIDTH + cw + d) * d * 2 + 4 * tm * MIX_SUB * 4)
    return pl.pallas_call(
        functools.partial(_mix_kernel, seq // tm),
        out_shape=jax.ShapeDtypeStruct((m, d), F32),
        grid=(m // tm,),
        in_specs=[
            pl.BlockSpec((tm, ATTN_WIDTH), lambda i: (i, 0)),
            pl.BlockSpec((tm, cw), lambda i: (i, Z_CB // cw)),
            pl.BlockSpec((tm, cw), lambda i: (i, Z_U // cw)),
            pl.BlockSpec((SUBLANES, cw), lambda i: (jnp.maximum(i * halo_blocks - 1, 0), Z_U // cw)),
            pl.BlockSpec((CONV_K, cw), lambda i: (0, 0)),
            pl.BlockSpec((tm, d), lambda i: (i, Z_GA // d)),
            pl.BlockSpec((tm, d), lambda i: (i, Z_GC // d)),
            pl.BlockSpec((tm, d), lambda i: (i, 0)),
            _resident((ATTN_WIDTH, d)),
            _resident((cw, d)),
            _resident((d, d)),
        ],
        out_specs=pl.BlockSpec((tm, d), lambda i: (i, 0)),
        scratch_shapes=[pltpu.VMEM((tm + SUBLANES, cw), F32), pltpu.VMEM((tm, cw), BF16),
                        pltpu.VMEM((tm, d), BF16)],
        compiler_params=pltpu.CompilerParams(
            dimension_semantics=("arbitrary",), vmem_limit_bytes=vmem + 6 * MIB),
        name="mix",
    )(attn2, z2, z2, z2, w_conv, z2, z2, x2, w_attn_out, w_conv_out, w_o)


def _mlp_kernel(r_ref, g_ref, wu_ref, wd_ref, o_ref, h_ref):
    f = pl.program_id(1)

    @pl.when(f == 0)
    def _():
        r = r_ref[...]
        h_ref[...] = (_rms_normalize(r) * g_ref[...]).astype(BF16)
        o_ref[...] = r

    a = jnp.maximum(_dot(h_ref[...], wu_ref[...]), 0.0)
    o_ref[...] += _dot((a * a).astype(BF16), wd_ref[...])


def _mlp(r, g_mlp, w_up, w_down, *, tm, tf):
    m = r.shape[0]
    vmem = (4 * tm * D_MODEL * 4 + tm * D_MODEL * 2 + 4 * D_MODEL * tf * 2 + 2 * tm * tf * 4)
    return pl.pallas_call(
        _mlp_kernel,
        out_shape=jax.ShapeDtypeStruct((m, D_MODEL), F32),
        grid=(m // tm, D_FF // tf),
        in_specs=[
            pl.BlockSpec((tm, D_MODEL), lambda i, f: (i, 0)),
            pl.BlockSpec((1, D_MODEL), lambda i, f: (0, 0)),
            pl.BlockSpec((D_MODEL, tf), lambda i, f: (0, f)),
            pl.BlockSpec((tf, D_MODEL), lambda i, f: (f, 0)),
        ],
        out_specs=pl.BlockSpec((tm, D_MODEL), lambda i, f: (i, 0)),
        scratch_shapes=[pltpu.VMEM((tm, D_MODEL), BF16)],
        compiler_params=pltpu.CompilerParams(
            dimension_semantics=("arbitrary", "arbitrary"), vmem_limit_bytes=vmem + 8 * MIB),
        name="mlp",
    )(r, g_mlp, w_up, w_down)


def _ple_kernel(r_ref, g_ref, p_ref, wg_ref, wp_ref, o_ref, h_ref):
    h_ref[...] = (_rms_normalize(r_ref[...]) * g_ref[...]).astype(BF16)
    pb = p_ref[...].astype(BF16)
    for c in range(D_MODEL // MIX_SUB):
        sl = slice(c * MIX_SUB, (c + 1) * MIX_SUB)
        gate = _sigmoid(_dot(h_ref[...], wg_ref[:, sl]))
        o_ref[:, sl] = r_ref[:, sl] + gate * _dot(pb, wp_ref[:, sl])


def _ple(r, g_ple, p2, w_gate, w_proj, *, tm):
    m = r.shape[0]
    d = D_MODEL
    vmem = (4 * tm * d * 4 + tm * d * 2 + 2 * tm * PLE_DIM * 4 + tm * PLE_DIM * 2
            + (d + PLE_DIM) * d * 2 + 4 * tm * MIX_SUB * 4)
    return pl.pallas_call(
        _ple_kernel,
        out_shape=jax.ShapeDtypeStruct((m, d), F32),
        grid=(m // tm,),
        in_specs=[
            pl.BlockSpec((tm, d), lambda i: (i, 0)),
            pl.BlockSpec((1, d), lambda i: (0, 0)),
            pl.BlockSpec((tm, PLE_DIM), lambda i: (i, 0)),
            _resident((d, d)),
            _resident((PLE_DIM, d)),
        ],
        out_specs=pl.BlockSpec((tm, d), lambda i: (i, 0)),
        scratch_shapes=[pltpu.VMEM((tm, d), BF16)],
        compiler_params=pltpu.CompilerParams(
            dimension_semantics=("arbitrary",), vmem_limit_bytes=vmem + 6 * MIB),
        name="ple",
    )(r, g_ple, p2, w_gate, w_proj)


def kernel(x, p, g_mix, w_in, g_q, g_k, w_conv, w_attn_out, w_conv_out, w_o, g_mlp, w_up, w_down,
           g_ple, w_ple_gate, w_ple_proj):
    b, s, d = x.shape
    depth = p.shape[0]
    m = b * s
    r = x.reshape(m, d)
    for i in range(depth):
        z = _in_proj(r, g_mix[i][None], g_q[i][None], g_k[i][None], w_in[i], tm=1024)
        attn = _moba(z.reshape(b, s, Z_COLS)).reshape(m, ATTN_WIDTH)
        r = _mix(attn, z, r, w_conv[i], w_attn_out[i].astype(BF16), w_conv_out[i].astype(BF16),
                 w_o[i].astype(BF16), tm=512, seq=s)
        r = _mlp(r, g_mlp[i][None], w_up[i].astype(BF16), w_down[i].astype(BF16), tm=512, tf=1024)
        r = _ple(r, g_ple[i][None], p[i].reshape(m, PLE_DIM), w_ple_gate[i].astype(BF16),
                 w_ple_proj[i].astype(BF16), tm=512)
    return r.reshape(b, s, d)
```

```python
import functools
import math

import jax
import jax.numpy as jnp
from jax import lax
from jax.experimental import pallas as pl
from jax.experimental.pallas import tpu as pltpu

F32 = jnp.float32
BF16 = jnp.bfloat16

D_MODEL = 2048
N_HEADS = 8
HEAD_DIM = 128
ATTN_WIDTH = N_HEADS * HEAD_DIM
CONV_WIDTH = 1024
CONV_K = 3
MOBA_BLOCK = 256
MOBA_TOP_K = 3
D_FF = 4 * D_MODEL
PLE_DIM = 256
RMS_EPS = 1e-6
IN_COLS = 3 * ATTN_WIDTH + 3 * CONV_WIDTH + 2 * D_MODEL
Q_PRESCALE = HEAD_DIM ** -0.5 * math.log2(math.e)

LANES = 128
SUBLANES = 8
MIB = 1024 * 1024

IN_TN = 1024
IN_SPLITS = (0, 768, 1024)
_STEP_Q, _STEP_K, _STEP_V, _STEP_CB, _STEP_CC, _STEP_CX, _STEP_GATES = 0, 1, 2, 3, 4, 5, 6
_IN_STEPS = IN_COLS // IN_TN
MIX_SUB = 512
Z_COLS = IN_COLS - CONV_WIDTH
Z_GA, Z_GC = 0, D_MODEL
Z_Q = 2 * D_MODEL
Z_K = Z_Q + ATTN_WIDTH
Z_V = Z_K + ATTN_WIDTH
Z_CB = Z_V + ATTN_WIDTH
Z_U = Z_CB + CONV_WIDTH


def _rms_normalize(x):
    return x * lax.rsqrt(jnp.mean(x * x, axis=-1, keepdims=True) + RMS_EPS)


def _sigmoid(x):
    return 0.5 * jnp.tanh(0.5 * x) + 0.5


def _dot(a, b):
    return jnp.dot(a, b, preferred_element_type=F32)


def _dot_nt(a, b):
    return lax.dot_general(a, b, (((1,), (1,)), ((), ())), preferred_element_type=F32)


def _in_proj_kernel(x_ref, gmix_ref, gq_ref, gk_ref, w_ref, z_ref, h_ref, cc_ref):
    j = pl.program_id(1)

    @pl.when(j == 0)
    def _():
        h_ref[...] = (_rms_normalize(x_ref[...]) * gmix_ref[...]).astype(BF16)

    def for_each_sub(epilogue):
        for lo, hi in zip(IN_SPLITS[:-1], IN_SPLITS[1:]):
            sl = slice(lo, hi)
            epilogue(sl, _dot(h_ref[...], w_ref[:, sl].astype(BF16)))

    def head_norm(gain):
        def epilogue(sl, acc):
            for c in range((sl.stop - sl.start) // HEAD_DIM):
                hs = slice(c * HEAD_DIM, (c + 1) * HEAD_DIM)
                z_ref[:, sl.start + hs.start:sl.start + hs.stop] = (
                    _rms_normalize(acc[:, hs]) * gain).astype(BF16)
        return epilogue

    @pl.when(j == _STEP_Q)
    def _():
        for_each_sub(head_norm(gq_ref[...] * Q_PRESCALE))

    @pl.when(j == _STEP_K)
    def _():
        for_each_sub(head_norm(gk_ref[...]))

    @pl.when((j == _STEP_V) | (j == _STEP_CB))
    def _():
        def epilogue(sl, acc):
            z_ref[:, sl] = acc.astype(BF16)
        for_each_sub(epilogue)

    @pl.when(j == _STEP_CC)
    def _():
        def epilogue(sl, acc):
            cc_ref[:, sl] = acc
        for_each_sub(epilogue)

    @pl.when(j == _STEP_CX)
    def _():
        def epilogue(sl, acc):
            z_ref[:, sl] = (cc_ref[:, sl] * acc).astype(BF16)
        for_each_sub(epilogue)

    @pl.when(j >= _STEP_GATES)
    def _():
        def epilogue(sl, acc):
            z_ref[:, sl] = _sigmoid(acc).astype(BF16)
        for_each_sub(epilogue)


def _in_proj(x2, g_mix, g_q, g_k, w_in, *, tm):
    m = x2.shape[0]

    def z_index(i, j):
        attn_conv = Z_Q // IN_TN + jnp.where(j < _STEP_CC, j, j - 1)
        return i, jnp.where(j < _STEP_GATES, attn_conv, j - _STEP_GATES)

    vmem = (2 * tm * D_MODEL * 4 + tm * D_MODEL * 2 + 2 * D_MODEL * IN_TN * 4 + D_MODEL * IN_TN * 2
            + 2 * tm * IN_TN * 2 + tm * IN_TN * 4 + tm * IN_TN * 4)
    return pl.pallas_call(
        _in_proj_kernel,
        out_shape=jax.ShapeDtypeStruct((m, Z_COLS), BF16),
        grid=(m // tm, _IN_STEPS),
        in_specs=[
            pl.BlockSpec((tm, D_MODEL), lambda i, j: (i, 0)),
            pl.BlockSpec((1, D_MODEL), lambda i, j: (0, 0)),
            pl.BlockSpec((1, HEAD_DIM), lambda i, j: (0, 0)),
            pl.BlockSpec((1, HEAD_DIM), lambda i, j: (0, 0)),
            pl.BlockSpec((D_MODEL, IN_TN), lambda i, j: (0, j)),
        ],
        out_specs=pl.BlockSpec((tm, IN_TN), z_index),
        scratch_shapes=[pltpu.VMEM((tm, D_MODEL), BF16), pltpu.VMEM((tm, IN_TN), F32)],
        compiler_params=pltpu.CompilerParams(
            dimension_semantics=("arbitrary", "arbitrary"), vmem_limit_bytes=vmem + 6 * MIB),
        name="in_proj",
    )(x2, g_mix, g_q, g_k, w_in)


def _moba_kernel(n_cast, q_ref, k_ref, v_ref, *refs):
    w32_refs, o_ref, w16_refs = refs[:n_cast], refs[n_cast], refs[n_cast + 1:2 * n_cast + 1]
    kmean_ref, vt_ref, bias_ref, m_ref, l_ref, acc_ref, sa_ref, sb_ref = refs[2 * n_cast + 1:]
    j = pl.program_id(1)
    nb = k_ref.shape[1] // MOBA_BLOCK
    heads = k_ref.shape[2] // HEAD_DIM
    neg_inf = F32(-jnp.inf)

    for w32_ref, w16_ref in zip(w32_refs, w16_refs):
        w16_ref[...] = w32_ref[...].astype(BF16)

    def head_cols(h):
        return slice(h * HEAD_DIM, (h + 1) * HEAD_DIM)

    @pl.when(j == 0)
    def _():
        for h in range(heads):
            for n in range(nb):
                blk = slice(n * MOBA_BLOCK, (n + 1) * MOBA_BLOCK)
                kmean_ref[h, n:n + 1, :] = jnp.mean(
                    k_ref[0, blk, head_cols(h)].astype(F32), axis=0, keepdims=True)
                vt_ref[head_cols(h), blk] = v_ref[0, blk, head_cols(h)].T

    for h in range(heads):
        q = q_ref[0, :, head_cols(h)]
        kmean = kmean_ref[h]
        km_hi = kmean.astype(BF16)
        km_lo = (kmean - km_hi.astype(F32)).astype(BF16)
        gate = _dot_nt(km_hi, q) + _dot_nt(km_lo, q)
        blk_id = lax.broadcasted_iota(jnp.int32, gate.shape, 0)
        gate = jnp.where(blk_id < j, gate, neg_inf)
        sel = jnp.zeros(gate.shape, jnp.bool_)
        for t in range(MOBA_TOP_K):
            best = jnp.max(gate, axis=0, keepdims=True)
            first = jnp.min(jnp.where(gate == best, blk_id, nb), axis=0, keepdims=True)
            pick = blk_id == first
            sel = sel | (pick & (t < j))
            gate = jnp.where(pick, neg_inf, gate)
        bias_ref[h] = jnp.where(sel, F32(0), neg_inf)

    def scores(n, h):
        off = pl.multiple_of(n * MOBA_BLOCK, MOBA_BLOCK)
        return _dot_nt(k_ref[0, pl.ds(off, MOBA_BLOCK), head_cols(h)], q_ref[0, :, head_cols(h)])

    def attend(n, cur_ref, nxt_ref, n_next, own):
        off = pl.multiple_of(n * MOBA_BLOCK, MOBA_BLOCK)
        if nxt_ref is not None:
            n_next = jnp.clip(n_next, 0, jnp.maximum(j - 1, 0))
        for h in range(heads):
            if nxt_ref is not None:
                nxt_ref[h] = scores(n_next, h)
            s = cur_ref[h]
            if own:
                key_i = lax.broadcasted_iota(jnp.int32, s.shape, 0)
                qry_i = lax.broadcasted_iota(jnp.int32, s.shape, 1)
                s = jnp.where(key_i <= qry_i, s, neg_inf)
                m_new = jnp.max(s, axis=0, keepdims=True)
                shift, alpha = m_new, None
            else:
                bias = bias_ref[h, pl.ds(n, 1), :]
                m_old = m_ref[h]
                m_new = jnp.maximum(m_old, jnp.max(s, axis=0, keepdims=True) + bias)
                alpha = jnp.exp2(m_old - m_new)
                shift = m_new - bias
            p = jnp.exp2(s - shift)
            l_blk = jnp.sum(p, axis=0, keepdims=True)
            m_ref[h] = m_new
            l_ref[h] = l_blk if own else alpha * l_ref[h] + l_blk
            pv = _dot(vt_ref[head_cols(h), pl.ds(off, MOBA_BLOCK)], p.astype(BF16))
            acc_ref[h] = pv if own else alpha * acc_ref[h] + pv

    for h in range(heads):
        sa_ref[h] = scores(j, h)
    attend(j, sa_ref, sb_ref, 0, True)

    def body(i, carry):
        attend(2 * i, sb_ref, sa_ref, 2 * i + 1, False)
        attend(2 * i + 1, sa_ref, sb_ref, 2 * i + 2, False)
        return carry

    lax.fori_loop(0, j // 2, body, 0)

    @pl.when(j % 2 == 1)
    def _():
        attend(j - 1, sb_ref, None, None, False)

    for h in range(heads):
        o_ref[0, :, head_cols(h)] = (acc_ref[h] / l_ref[h]).T.astype(BF16)


def _moba(z3, weights_f32):
    b, s, _ = z3.shape
    nq = s // MOBA_BLOCK
    aw = ATTN_WIDTH
    steps = b * nq
    shapes = [w.shape for w in weights_f32]
    folded = []
    for w in weights_f32:
        rows, cols = w.shape
        while rows // steps < 2 * SUBLANES:
            rows, cols = rows * 2, cols // 2
        folded.append(w.reshape(rows, cols))
    chunk_specs = [pl.BlockSpec((w.shape[0] // steps, w.shape[1]), lambda bi, j: (bi * nq + j, 0))
                   for w in folded]
    cast_bytes = sum(w.size // steps for w in folded) * 2 * (4 + 2)
    vmem = (2 * s * aw * 2 + s * aw * 2 + 4 * MOBA_BLOCK * aw * 2 + cast_bytes
            + N_HEADS * (HEAD_DIM + 2 * nq + 2 * SUBLANES + 2 * MOBA_BLOCK) * MOBA_BLOCK * 4)
    kv_once = pl.Buffered(1)
    outs = pl.pallas_call(
        functools.partial(_moba_kernel, len(folded)),
        out_shape=[jax.ShapeDtypeStruct((b, s, aw), BF16)]
        + [jax.ShapeDtypeStruct(w.shape, BF16) for w in folded],
        grid=(b, nq),
        in_specs=[
            pl.BlockSpec((1, MOBA_BLOCK, aw), lambda bi, j: (bi, j, Z_Q // aw)),
            pl.BlockSpec((1, s, aw), lambda bi, j: (bi, 0, Z_K // aw), pipeline_mode=kv_once),
            pl.BlockSpec((1, s, aw), lambda bi, j: (bi, 0, Z_V // aw), pipeline_mode=kv_once),
        ] + chunk_specs,
        out_specs=[pl.BlockSpec((1, MOBA_BLOCK, aw), lambda bi, j: (bi, j, 0))] + chunk_specs,
        scratch_shapes=[
            pltpu.VMEM((N_HEADS, nq, HEAD_DIM), F32),
            pltpu.VMEM((aw, s), BF16),
            pltpu.VMEM((N_HEADS, nq, MOBA_BLOCK), F32),
            pltpu.VMEM((N_HEADS, 1, MOBA_BLOCK), F32),
            pltpu.VMEM((N_HEADS, 1, MOBA_BLOCK), F32),
            pltpu.VMEM((N_HEADS, HEAD_DIM, MOBA_BLOCK), F32),
            pltpu.VMEM((N_HEADS, MOBA_BLOCK, MOBA_BLOCK), F32),
            pltpu.VMEM((N_HEADS, MOBA_BLOCK, MOBA_BLOCK), F32),
        ],
        compiler_params=pltpu.CompilerParams(
            dimension_semantics=("arbitrary", "arbitrary"), vmem_limit_bytes=vmem + 6 * MIB),
        name="moba",
    )(z3, z3, z3, *folded)
    return outs[0], [w.reshape(shape) for w, shape in zip(outs[1:], shapes)]


def _mix_kernel(seq_tiles, attn_ref, cb_ref, u_ref, halo_ref, wconv_ref, sa_ref, sc_ref, x_ref,
                wa_ref, wc_ref, wo_ref, o_ref, ubuf_ref, cin_ref, mg_ref):
    i = pl.program_id(0)
    tm = attn_ref.shape[0]

    halo = halo_ref[...].astype(F32)
    ubuf_ref[0:SUBLANES, :] = jnp.where(i % seq_tiles == 0, jnp.zeros_like(halo), halo)
    ubuf_ref[SUBLANES:, :] = u_ref[...].astype(F32)
    w = wconv_ref[...]
    conv = (w[0:1, :] * ubuf_ref[SUBLANES - 2:SUBLANES - 2 + tm, :]
            + w[1:2, :] * ubuf_ref[SUBLANES - 1:SUBLANES - 1 + tm, :]
            + w[2:3, :] * ubuf_ref[SUBLANES:, :])
    cin_ref[...] = (cb_ref[...].astype(F32) * conv).astype(BF16)

    for c in range(D_MODEL // MIX_SUB):
        sl = slice(c * MIX_SUB, (c + 1) * MIX_SUB)
        y_attn = _dot(attn_ref[...], wa_ref[:, sl])
        y_conv = _dot(cin_ref[...], wc_ref[:, sl])
        mg_ref[:, sl] = (sa_ref[:, sl].astype(F32) * y_attn
                         + sc_ref[:, sl].astype(F32) * y_conv).astype(BF16)
    for c in range(D_MODEL // MIX_SUB):
        sl = slice(c * MIX_SUB, (c + 1) * MIX_SUB)
        o_ref[:, sl] = x_ref[:, sl] + _dot(mg_ref[...], wo_ref[:, sl])


def _resident(shape):
    return pl.BlockSpec(shape, lambda i: (0,) * len(shape), pipeline_mode=pl.Buffered(1))


def _mix(attn2, z2, x2, w_conv, w_attn_out, w_conv_out, w_o, *, tm, seq):
    m = attn2.shape[0]
    cw, d = CONV_WIDTH, D_MODEL
    halo_blocks = tm // SUBLANES
    vmem = (2 * tm * ATTN_WIDTH * 2 + 4 * tm * cw * 2 + 4 * tm * d * 2 + 4 * tm * d * 4
            + (tm + SUBLANES) * cw * 4 + tm * cw * 2 + tm * d * 2
            + (ATTN_WIDTH + cw + d) * d * 2 + 4 * tm * MIX_SUB * 4)
    return pl.pallas_call(
        functools.partial(_mix_kernel, seq // tm),
        out_shape=jax.ShapeDtypeStruct((m, d), F32),
        grid=(m // tm,),
        in_specs=[
            pl.BlockSpec((tm, ATTN_WIDTH), lambda i: (i, 0)),
            pl.BlockSpec((tm, cw), lambda i: (i, Z_CB // cw)),
            pl.BlockSpec((tm, cw), lambda i: (i, Z_U // cw)),
            pl.BlockSpec((SUBLANES, cw), lambda i: (jnp.maximum(i * halo_blocks - 1, 0), Z_U // cw)),
            pl.BlockSpec((CONV_K, cw), lambda i: (0, 0)),
            pl.BlockSpec((tm, d), lambda i: (i, Z_GA // d)),
            pl.BlockSpec((tm, d), lambda i: (i, Z_GC // d)),
            pl.BlockSpec((tm, d), lambda i: (i, 0)),
            _resident((ATTN_WIDTH, d)),
            _resident((cw, d)),
            _resident((d, d)),
        ],
        out_specs=pl.BlockSpec((tm, d), lambda i: (i, 0)),
        scratch_shapes=[pltpu.VMEM((tm + SUBLANES, cw), F32), pltpu.VMEM((tm, cw), BF16),
                        pltpu.VMEM((tm, d), BF16)],
        compiler_params=pltpu.CompilerParams(
            dimension_semantics=("arbitrary",), vmem_limit_bytes=vmem + 6 * MIB),
        name="mix",
    )(attn2, z2, z2, z2, w_conv, z2, z2, x2, w_attn_out, w_conv_out, w_o)


def _mlp_kernel(r_ref, g_ref, wu_ref, wd_ref, o_ref, h_ref):
    f = pl.program_id(1)

    @pl.when(f == 0)
    def _():
        r = r_ref[...]
        h_ref[...] = (_rms_normalize(r) * g_ref[...]).astype(BF16)
        o_ref[...] = r

    a = jnp.maximum(_dot(h_ref[...], wu_ref[...]), 0.0)
    o_ref[...] += _dot((a * a).astype(BF16), wd_ref[...])


def _mlp(r, g_mlp, w_up, w_down, *, tm, tf):
    m = r.shape[0]
    vmem = (4 * tm * D_MODEL * 4 + tm * D_MODEL * 2 + 4 * D_MODEL * tf * 2 + 2 * tm * tf * 4)
    return pl.pallas_call(
        _mlp_kernel,
        out_shape=jax.ShapeDtypeStruct((m, D_MODEL), F32),
        grid=(m // tm, D_FF // tf),
        in_specs=[
            pl.BlockSpec((tm, D_MODEL), lambda i, f: (i, 0)),
            pl.BlockSpec((1, D_MODEL), lambda i, f: (0, 0)),
            pl.BlockSpec((D_MODEL, tf), lambda i, f: (0, f)),
            pl.BlockSpec((tf, D_MODEL), lambda i, f: (f, 0)),
        ],
        out_specs=pl.BlockSpec((tm, D_MODEL), lambda i, f: (i, 0)),
        scratch_shapes=[pltpu.VMEM((tm, D_MODEL), BF16)],
        compiler_params=pltpu.CompilerParams(
            dimension_semantics=("arbitrary", "arbitrary"), vmem_limit_bytes=vmem + 8 * MIB),
        name="mlp",
    )(r, g_mlp, w_up, w_down)


def _ple_kernel(r_ref, g_ref, p_ref, wg_ref, wp_ref, o_ref, h_ref):
    h_ref[...] = (_rms_normalize(r_ref[...]) * g_ref[...]).astype(BF16)
    pb = p_ref[...].astype(BF16)
    for c in range(D_MODEL // MIX_SUB):
        sl = slice(c * MIX_SUB, (c + 1) * MIX_SUB)
        gate = _sigmoid(_dot(h_ref[...], wg_ref[:, sl]))
        o_ref[:, sl] = r_ref[:, sl] + gate * _dot(pb, wp_ref[:, sl])


def _ple(r, g_ple, p2, w_gate, w_proj, *, tm):
    m = r.shape[0]
    d = D_MODEL
    vmem = (4 * tm * d * 4 + tm * d * 2 + 2 * tm * PLE_DIM * 4 + tm * PLE_DIM * 2
            + (d + PLE_DIM) * d * 2 + 4 * tm * MIX_SUB * 4)
    return pl.pallas_call(
        _ple_kernel,
        out_shape=jax.ShapeDtypeStruct((m, d), F32),
        grid=(m // tm,),
        in_specs=[
            pl.BlockSpec((tm, d), lambda i: (i, 0)),
            pl.BlockSpec((1, d), lambda i: (0, 0)),
            pl.BlockSpec((tm, PLE_DIM), lambda i: (i, 0)),
            _resident((d, d)),
            _resident((PLE_DIM, d)),
        ],
        out_specs=pl.BlockSpec((tm, d), lambda i: (i, 0)),
        scratch_shapes=[pltpu.VMEM((tm, d), BF16)],
        compiler_params=pltpu.CompilerParams(
            dimension_semantics=("arbitrary",), vmem_limit_bytes=vmem + 6 * MIB),
        name="ple",
    )(r, g_ple, p2, w_gate, w_proj)


def kernel(x, p, g_mix, w_in, g_q, g_k, w_conv, w_attn_out, w_conv_out, w_o, g_mlp, w_up, w_down,
           g_ple, w_ple_gate, w_ple_proj):
    b, s, d = x.shape
    depth = p.shape[0]
    m = b * s
    r = x.reshape(m, d)
    for i in range(depth):
        z = _in_proj(r, g_mix[i][None], g_q[i][None], g_k[i][None], w_in[i], tm=1024)
        attn, (wa, wc, wo, wu, wd, wg, wp) = _moba(
            z.reshape(b, s, Z_COLS),
            [w_attn_out[i], w_conv_out[i], w_o[i], w_up[i], w_down[i], w_ple_gate[i], w_ple_proj[i]])
        r = _mix(attn.reshape(m, ATTN_WIDTH), z, r, w_conv[i], wa, wc, wo, tm=512, seq=s)
        r = _mlp(r, g_mlp[i][None], wu, wd, tm=512, tf=1024)
        r = _ple(r, g_ple[i][None], p[i].reshape(m, PLE_DIM), wg, wp, tm=512)
    return r.reshape(b, s, d)
```

```python
import functools
import math

import jax
import jax.numpy as jnp
from jax import lax
from jax.experimental import pallas as pl
from jax.experimental.pallas import tpu as pltpu

F32 = jnp.float32
BF16 = jnp.bfloat16

D_MODEL = 2048
N_HEADS = 8
HEAD_DIM = 128
ATTN_WIDTH = N_HEADS * HEAD_DIM
CONV_WIDTH = 1024
CONV_K = 3
MOBA_BLOCK = 256
MOBA_TOP_K = 3
MOBA_ONES = 16
D_FF = 4 * D_MODEL
PLE_DIM = 256
RMS_EPS = 1e-6
IN_COLS = 3 * ATTN_WIDTH + 3 * CONV_WIDTH + 2 * D_MODEL
Q_PRESCALE = HEAD_DIM ** -0.5 * math.log2(math.e)

LANES = 128
SUBLANES = 8
MIB = 1024 * 1024

IN_TN = 1024
IN_SPLITS = (0, 768, 1024)
_STEP_Q, _STEP_K, _STEP_V, _STEP_CB, _STEP_CC, _STEP_CX, _STEP_GATES = 0, 1, 2, 3, 4, 5, 6
_IN_STEPS = IN_COLS // IN_TN
MIX_SUB = 512
Z_COLS = IN_COLS - CONV_WIDTH
Z_GA, Z_GC = 0, D_MODEL
Z_Q = 2 * D_MODEL
Z_K = Z_Q + ATTN_WIDTH
Z_V = Z_K + ATTN_WIDTH
Z_CB = Z_V + ATTN_WIDTH
Z_U = Z_CB + CONV_WIDTH


def _rms_normalize(x):
    return x * lax.rsqrt(jnp.mean(x * x, axis=-1, keepdims=True) + RMS_EPS)


def _sigmoid(x):
    return 0.5 * jnp.tanh(0.5 * x) + 0.5


def _dot(a, b):
    return jnp.dot(a, b, preferred_element_type=F32)


def _dot_nt(a, b):
    return lax.dot_general(a, b, (((1,), (1,)), ((), ())), preferred_element_type=F32)


def _in_proj_kernel(x_ref, gmix_ref, gq_ref, gk_ref, w_ref, z_ref, h_ref, cc_ref):
    j = pl.program_id(1)

    @pl.when(j == 0)
    def _():
        h_ref[...] = (_rms_normalize(x_ref[...]) * gmix_ref[...]).astype(BF16)

    def for_each_sub(epilogue):
        for lo, hi in zip(IN_SPLITS[:-1], IN_SPLITS[1:]):
            sl = slice(lo, hi)
            epilogue(sl, _dot(h_ref[...], w_ref[:, sl].astype(BF16)))

    def head_norm(gain):
        def epilogue(sl, acc):
            for c in range((sl.stop - sl.start) // HEAD_DIM):
                hs = slice(c * HEAD_DIM, (c + 1) * HEAD_DIM)
                z_ref[:, sl.start + hs.start:sl.start + hs.stop] = (
                    _rms_normalize(acc[:, hs]) * gain).astype(BF16)
        return epilogue

    @pl.when(j == _STEP_Q)
    def _():
        for_each_sub(head_norm(gq_ref[...] * Q_PRESCALE))

    @pl.when(j == _STEP_K)
    def _():
        for_each_sub(head_norm(gk_ref[...]))

    @pl.when((j == _STEP_V) | (j == _STEP_CB))
    def _():
        def epilogue(sl, acc):
            z_ref[:, sl] = acc.astype(BF16)
        for_each_sub(epilogue)

    @pl.when(j == _STEP_CC)
    def _():
        def epilogue(sl, acc):
            cc_ref[:, sl] = acc
        for_each_sub(epilogue)

    @pl.when(j == _STEP_CX)
    def _():
        def epilogue(sl, acc):
            z_ref[:, sl] = (cc_ref[:, sl] * acc).astype(BF16)
        for_each_sub(epilogue)

    @pl.when(j >= _STEP_GATES)
    def _():
        def epilogue(sl, acc):
            z_ref[:, sl] = _sigmoid(acc).astype(BF16)
        for_each_sub(epilogue)


def _in_proj(x2, g_mix, g_q, g_k, w_in, *, tm):
    m = x2.shape[0]

    def z_index(i, j):
        attn_conv = Z_Q // IN_TN + jnp.where(j < _STEP_CC, j, j - 1)
        return i, jnp.where(j < _STEP_GATES, attn_conv, j - _STEP_GATES)

    vmem = (2 * tm * D_MODEL * 4 + tm * D_MODEL * 2 + 2 * D_MODEL * IN_TN * 4 + D_MODEL * IN_TN * 2
            + 2 * tm * IN_TN * 2 + tm * IN_TN * 4 + tm * IN_TN * 4)
    return pl.pallas_call(
        _in_proj_kernel,
        out_shape=jax.ShapeDtypeStruct((m, Z_COLS), BF16),
        grid=(m // tm, _IN_STEPS),
        in_specs=[
            pl.BlockSpec((tm, D_MODEL), lambda i, j: (i, 0)),
            pl.BlockSpec((1, D_MODEL), lambda i, j: (0, 0)),
            pl.BlockSpec((1, HEAD_DIM), lambda i, j: (0, 0)),
            pl.BlockSpec((1, HEAD_DIM), lambda i, j: (0, 0)),
            pl.BlockSpec((D_MODEL, IN_TN), lambda i, j: (0, j)),
        ],
        out_specs=pl.BlockSpec((tm, IN_TN), z_index),
        scratch_shapes=[pltpu.VMEM((tm, D_MODEL), BF16), pltpu.VMEM((tm, IN_TN), F32)],
        compiler_params=pltpu.CompilerParams(
            dimension_semantics=("arbitrary", "arbitrary"), vmem_limit_bytes=vmem + 6 * MIB),
        name="in_proj",
    )(x2, g_mix, g_q, g_k, w_in)


def _moba_kernel(n_cast, q_ref, k_ref, v_ref, *refs):
    w32_refs, o_ref, w16_refs = refs[:n_cast], refs[n_cast], refs[n_cast + 1:2 * n_cast + 1]
    kmean_ref, vt_ref, qt_ref, bias_ref, m_ref, acc_ref, sa_ref, sb_ref = refs[2 * n_cast + 1:]
    j = pl.program_id(1)
    nb = k_ref.shape[1] // MOBA_BLOCK
    heads = k_ref.shape[2] // HEAD_DIM
    neg_inf = F32(-jnp.inf)

    def head_cols(h):
        return slice(h * HEAD_DIM, (h + 1) * HEAD_DIM)

    def vt_rows(h):
        return slice(h * (HEAD_DIM + MOBA_ONES), (h + 1) * (HEAD_DIM + MOBA_ONES))

    @pl.when(j == 0)
    def _():
        for h in range(heads):
            base = h * (HEAD_DIM + MOBA_ONES)
            vt_ref[base + HEAD_DIM:base + HEAD_DIM + MOBA_ONES, :] = jnp.ones(
                (MOBA_ONES, vt_ref.shape[1]), BF16)
            for n in range(nb):
                blk = slice(n * MOBA_BLOCK, (n + 1) * MOBA_BLOCK)
                kmean_ref[h, n:n + 1, :] = jnp.mean(
                    k_ref[0, blk, head_cols(h)].astype(F32), axis=0, keepdims=True)
                vt_ref[base:base + HEAD_DIM, blk] = v_ref[0, blk, head_cols(h)].T

    for h in range(heads):
        q = q_ref[0, :, head_cols(h)]
        qt_ref[h] = q.T
        kmean = kmean_ref[h]
        km_hi = kmean.astype(BF16)
        km_lo = (kmean - km_hi.astype(F32)).astype(BF16)
        gate = _dot_nt(km_hi, q) + _dot_nt(km_lo, q)
        blk_id = lax.broadcasted_iota(jnp.int32, gate.shape, 0)
        gate = jnp.where(blk_id < j, gate, neg_inf)
        sel = jnp.zeros(gate.shape, jnp.bool_)
        for t in range(MOBA_TOP_K):
            best = jnp.max(gate, axis=0, keepdims=True)
            first = jnp.min(jnp.where(gate == best, blk_id, nb), axis=0, keepdims=True)
            pick = blk_id == first
            sel = sel | (pick & (t < j))
            gate = jnp.where(pick, neg_inf, gate)
        bias_ref[h] = jnp.where(sel, F32(0), neg_inf)

    def scores(n, h):
        off = pl.multiple_of(n * MOBA_BLOCK, MOBA_BLOCK)
        return _dot(k_ref[0, pl.ds(off, MOBA_BLOCK), head_cols(h)], qt_ref[h])

    def attend(n, cur_ref, nxt_ref, n_next, own):
        off = pl.multiple_of(n * MOBA_BLOCK, MOBA_BLOCK)
        if nxt_ref is not None:
            n_next = jnp.clip(n_next, 0, jnp.maximum(j - 1, 0))
        for h in range(heads):
            if nxt_ref is not None:
                nxt_ref[h] = scores(n_next, h)
            s = cur_ref[h]
            if own:
                key_i = lax.broadcasted_iota(jnp.int32, s.shape, 0)
                qry_i = lax.broadcasted_iota(jnp.int32, s.shape, 1)
                s = jnp.where(key_i <= qry_i, s, neg_inf)
                m_new = jnp.max(s, axis=0, keepdims=True)
                shift, alpha = m_new, None
            else:
                bias = bias_ref[h, pl.ds(n, 1), :]
                m_old = m_ref[h]
                m_new = jnp.maximum(m_old, jnp.max(s, axis=0, keepdims=True) + bias)
                alpha = jnp.exp2(m_old - m_new)
                shift = m_new - bias
            p = jnp.exp2(s - shift)
            m_ref[h] = m_new
            pv = _dot(vt_ref[vt_rows(h), pl.ds(off, MOBA_BLOCK)], p.astype(BF16))
            acc_ref[h] = pv if own else alpha * acc_ref[h] + pv

    for h in range(heads):
        sa_ref[h] = scores(j, h)
    attend(j, sa_ref, sb_ref, 0, True)

    def body(i, carry):
        attend(2 * i, sb_ref, sa_ref, 2 * i + 1, False)
        attend(2 * i + 1, sa_ref, sb_ref, 2 * i + 2, False)
        return carry

    lax.fori_loop(0, j // 2, body, 0)

    @pl.when(j % 2 == 1)
    def _():
        attend(j - 1, sb_ref, None, None, False)

    for h in range(heads):
        out_t = acc_ref[h, 0:HEAD_DIM, :] / acc_ref[h, HEAD_DIM:HEAD_DIM + 1, :]
        o_ref[0, :, head_cols(h)] = out_t.T.astype(BF16)

    for w32_ref, w16_ref in zip(w32_refs, w16_refs):
        w16_ref[...] = w32_ref[...].astype(BF16)


def _moba(z3, weights_f32):
    b, s, _ = z3.shape
    nq = s // MOBA_BLOCK
    aw = ATTN_WIDTH
    steps = b * nq
    shapes = [w.shape for w in weights_f32]
    folded = []
    for w in weights_f32:
        rows, cols = w.shape
        while rows // steps < 2 * SUBLANES:
            rows, cols = rows * 2, cols // 2
        folded.append(w.reshape(rows, cols))
    chunk_specs = [pl.BlockSpec((w.shape[0] // steps, w.shape[1]), lambda bi, j: (bi * nq + j, 0))
                   for w in folded]
    cast_bytes = sum(w.size // steps for w in folded) * 2 * (4 + 2)
    acc_rows = HEAD_DIM + MOBA_ONES
    vmem = (2 * s * aw * 2 + N_HEADS * acc_rows * s * 2 + 5 * MOBA_BLOCK * aw * 2 + cast_bytes
            + N_HEADS * (acc_rows + 2 * nq + SUBLANES + 2 * MOBA_BLOCK) * MOBA_BLOCK * 4)
    kv_once = pl.Buffered(1)
    outs = pl.pallas_call(
        functools.partial(_moba_kernel, len(folded)),
        out_shape=[jax.ShapeDtypeStruct((b, s, aw), BF16)]
        + [jax.ShapeDtypeStruct(w.shape, BF16) for w in folded],
        grid=(b, nq),
        in_specs=[
            pl.BlockSpec((1, MOBA_BLOCK, aw), lambda bi, j: (bi, j, Z_Q // aw)),
            pl.BlockSpec((1, s, aw), lambda bi, j: (bi, 0, Z_K // aw), pipeline_mode=kv_once),
            pl.BlockSpec((1, s, aw), lambda bi, j: (bi, 0, Z_V // aw), pipeline_mode=kv_once),
        ] + chunk_specs,
        out_specs=[pl.BlockSpec((1, MOBA_BLOCK, aw), lambda bi, j: (bi, j, 0))] + chunk_specs,
        scratch_shapes=[
            pltpu.VMEM((N_HEADS, nq, HEAD_DIM), F32),
            pltpu.VMEM((N_HEADS * acc_rows, s), BF16),
            pltpu.VMEM((N_HEADS, HEAD_DIM, MOBA_BLOCK), BF16),
            pltpu.VMEM((N_HEADS, nq, MOBA_BLOCK), F32),
            pltpu.VMEM((N_HEADS, 1, MOBA_BLOCK), F32),
            pltpu.VMEM((N_HEADS, acc_rows, MOBA_BLOCK), F32),
            pltpu.VMEM((N_HEADS, MOBA_BLOCK, MOBA_BLOCK), F32),
            pltpu.VMEM((N_HEADS, MOBA_BLOCK, MOBA_BLOCK), F32),
        ],
        compiler_params=pltpu.CompilerParams(
            dimension_semantics=("arbitrary", "arbitrary"), vmem_limit_bytes=vmem + 6 * MIB),
        name="moba",
    )(z3, z3, z3, *folded)
    return outs[0], [w.reshape(shape) for w, shape in zip(outs[1:], shapes)]


def _mix_kernel(seq_tiles, attn_ref, cb_ref, u_ref, halo_ref, wconv_ref, sa_ref, sc_ref, x_ref,
                wa_ref, wc_ref, wo_ref, o_ref, ubuf_ref, cin_ref, mg_ref):
    i = pl.program_id(0)
    tm = attn_ref.shape[0]

    halo = halo_ref[...].astype(F32)
    ubuf_ref[0:SUBLANES, :] = jnp.where(i % seq_tiles == 0, jnp.zeros_like(halo), halo)
    ubuf_ref[SUBLANES:, :] = u_ref[...].astype(F32)
    w = wconv_ref[...]
    conv = (w[0:1, :] * ubuf_ref[SUBLANES - 2:SUBLANES - 2 + tm, :]
            + w[1:2, :] * ubuf_ref[SUBLANES - 1:SUBLANES - 1 + tm, :]
            + w[2:3, :] * ubuf_ref[SUBLANES:, :])
    cin_ref[...] = (cb_ref[...].astype(F32) * conv).astype(BF16)

    for c in range(D_MODEL // MIX_SUB):
        sl = slice(c * MIX_SUB, (c + 1) * MIX_SUB)
        y_attn = _dot(attn_ref[...], wa_ref[:, sl])
        y_conv = _dot(cin_ref[...], wc_ref[:, sl])
        mg_ref[:, sl] = (sa_ref[:, sl].astype(F32) * y_attn
                         + sc_ref[:, sl].astype(F32) * y_conv).astype(BF16)
    for c in range(D_MODEL // MIX_SUB):
        sl = slice(c * MIX_SUB, (c + 1) * MIX_SUB)
        o_ref[:, sl] = x_ref[:, sl] + _dot(mg_ref[...], wo_ref[:, sl])


def _resident(shape):
    return pl.BlockSpec(shape, lambda i: (0,) * len(shape), pipeline_mode=pl.Buffered(1))


def _mix(attn2, z2, x2, w_conv, w_attn_out, w_conv_out, w_o, *, tm, seq):
    m = attn2.shape[0]
    cw, d = CONV_WIDTH, D_MODEL
    halo_blocks = tm // SUBLANES
    vmem = (2 * tm * ATTN_WIDTH * 2 + 4 * tm * cw * 2 + 4 * tm * d * 2 + 4 * tm * d * 4
            + (tm + SUBLANES) * cw * 4 + tm * cw * 2 + tm * d * 2
            + (ATTN_WIDTH + cw + d) * d * 2 + 4 * tm * MIX_SUB * 4)
    return pl.pallas_call(
        functools.partial(_mix_kernel, seq // tm),
        out_shape=jax.ShapeDtypeStruct((m, d), F32),
        grid=(m // tm,),
        in_specs=[
            pl.BlockSpec((tm, ATTN_WIDTH), lambda i: (i, 0)),
            pl.BlockSpec((tm, cw), lambda i: (i, Z_CB // cw)),
            pl.BlockSpec((tm, cw), lambda i: (i, Z_U // cw)),
            pl.BlockSpec((SUBLANES, cw), lambda i: (jnp.maximum(i * halo_blocks - 1, 0), Z_U // cw)),
            pl.BlockSpec((CONV_K, cw), lambda i: (0, 0)),
            pl.BlockSpec((tm, d), lambda i: (i, Z_GA // d)),
            pl.BlockSpec((tm, d), lambda i: (i, Z_GC // d)),
            pl.BlockSpec((tm, d), lambda i: (i, 0)),
            _resident((ATTN_WIDTH, d)),
            _resident((cw, d)),
            _resident((d, d)),
        ],
        out_specs=pl.BlockSpec((tm, d), lambda i: (i, 0)),
        scratch_shapes=[pltpu.VMEM((tm + SUBLANES, cw), F32), pltpu.VMEM((tm, cw), BF16),
                        pltpu.VMEM((tm, d), BF16)],
        compiler_params=pltpu.CompilerParams(
            dimension_semantics=("arbitrary",), vmem_limit_bytes=vmem + 6 * MIB),
        name="mix",
    )(attn2, z2, z2, z2, w_conv, z2, z2, x2, w_attn_out, w_conv_out, w_o)


def _mlp_kernel(r_ref, g_ref, wu_ref, wd_ref, o_ref, h_ref):
    f = pl.program_id(1)

    @pl.when(f == 0)
    def _():
        r = r_ref[...]
        h_ref[...] = (_rms_normalize(r) * g_ref[...]).astype(BF16)
        o_ref[...] = r

    a = jnp.maximum(_dot(h_ref[...], wu_ref[...]), 0.0)
    o_ref[...] += _dot((a * a).astype(BF16), wd_ref[...])


def _mlp(r, g_mlp, w_up, w_down, *, tm, tf):
    m = r.shape[0]
    vmem = (4 * tm * D_MODEL * 4 + tm * D_MODEL * 2 + 4 * D_MODEL * tf * 2 + 2 * tm * tf * 4)
    return pl.pallas_call(
        _mlp_kernel,
        out_shape=jax.ShapeDtypeStruct((m, D_MODEL), F32),
        grid=(m // tm, D_FF // tf),
        in_specs=[
            pl.BlockSpec((tm, D_MODEL), lambda i, f: (i, 0)),
            pl.BlockSpec((1, D_MODEL), lambda i, f: (0, 0)),
            pl.BlockSpec((D_MODEL, tf), lambda i, f: (0, f)),
            pl.BlockSpec((tf, D_MODEL), lambda i, f: (f, 0)),
        ],
        out_specs=pl.BlockSpec((tm, D_MODEL), lambda i, f: (i, 0)),
        scratch_shapes=[pltpu.VMEM((tm, D_MODEL), BF16)],
        compiler_params=pltpu.CompilerParams(
            dimension_semantics=("arbitrary", "arbitrary"), vmem_limit_bytes=vmem + 8 * MIB),
        name="mlp",
    )(r, g_mlp, w_up, w_down)


def _ple_kernel(r_ref, g_ref, p_ref, wg_ref, wp_ref, o_ref, h_ref):
    h_ref[...] = (_rms_normalize(r_ref[...]) * g_ref[...]).astype(BF16)
    pb = p_ref[...].astype(BF16)
    for c in range(D_MODEL // MIX_SUB):
        sl = slice(c * MIX_SUB, (c + 1) * MIX_SUB)
        gate = _sigmoid(_dot(h_ref[...], wg_ref[:, sl]))
        o_ref[:, sl] = r_ref[:, sl] + gate * _dot(pb, wp_ref[:, sl])


def _ple(r, g_ple, p2, w_gate, w_proj, *, tm):
    m = r.shape[0]
    d = D_MODEL
    vmem = (4 * tm * d * 4 + tm * d * 2 + 2 * tm * PLE_DIM * 4 + tm * PLE_DIM * 2
            + (d + PLE_DIM) * d * 2 + 4 * tm * MIX_SUB * 4)
    return pl.pallas_call(
        _ple_kernel,
        out_shape=jax.ShapeDtypeStruct((m, d), F32),
        grid=(m // tm,),
        in_specs=[
            pl.BlockSpec((tm, d), lambda i: (i, 0)),
            pl.BlockSpec((1, d), lambda i: (0, 0)),
            pl.BlockSpec((tm, PLE_DIM), lambda i: (i, 0)),
            _resident((d, d)),
            _resident((PLE_DIM, d)),
        ],
        out_specs=pl.BlockSpec((tm, d), lambda i: (i, 0)),
        scratch_shapes=[pltpu.VMEM((tm, d), BF16)],
        compiler_params=pltpu.CompilerParams(
            dimension_semantics=("arbitrary",), vmem_limit_bytes=vmem + 6 * MIB),
        name="ple",
    )(r, g_ple, p2, w_gate, w_proj)


def kernel(x, p, g_mix, w_in, g_q, g_k, w_conv, w_attn_out, w_conv_out, w_o, g_mlp, w_up, w_down,
           g_ple, w_ple_gate, w_ple_proj):
    b, s, d = x.shape
    depth = p.shape[0]
    m = b * s
    r = x.reshape(m, d)
    for i in range(depth):
        z = _in_proj(r, g_mix[i][None], g_q[i][None], g_k[i][None], w_in[i], tm=1024)
        attn, (wa, wc, wo, wu, wd, wg, wp) = _moba(
            z.reshape(b, s, Z_COLS),
            [w_attn_out[i], w_conv_out[i], w_o[i], w_up[i], w_down[i], w_ple_gate[i], w_ple_proj[i]])
        r = _mix(attn.reshape(m, ATTN_WIDTH), z, r, w_conv[i], wa, wc, wo, tm=512, seq=s)
        r = _mlp(r, g_mlp[i][None], wu, wd, tm=512, tf=1024)
        r = _ple(r, g_ple[i][None], p[i].reshape(m, PLE_DIM), wg, wp, tm=512)
    return r.reshape(b, s, d)
```
